```python
import math
import jax, jax.numpy as jnp
from jax import lax
import numpy as np

D_MODEL = 1024
BATCH = 2
SEQ = 8192
DEPTH = 2

N_MIXERS = 2
CONV_WIDTH = 31
N_HEADS = 16
N_KV_HEADS = 2
HEAD_DIM = 64
GROUP = N_HEADS // N_KV_HEADS
WINDOW = 128
BLOCK = WINDOW
QKV_DIM = (N_HEADS + 2 * N_KV_HEADS) * HEAD_DIM
ATTN_DIM = N_HEADS * HEAD_DIM
D_FF = 4 * D_MODEL
REL_BUCKETS = 32
REL_MAX_DIST = 128
NORM_EPS = 1e-6
N_CONV_LAYERS = (DEPTH + 1) // 2
N_ATTN_LAYERS = DEPTH // 2
NEG_INF = -1e30

kernel_name = "hybrid_conformer_conv_swa_sink_trunk"


def rms_norm(x, g):
    xf = x.astype(jnp.float32)
    y = xf * lax.rsqrt(jnp.mean(xf * xf, axis=-1, keepdims=True) + NORM_EPS)
    return (y * g.astype(jnp.float32)).astype(x.dtype)


def layer_norm(x, g, b):
    xf = x.astype(jnp.float32)
    mu = jnp.mean(xf, axis=-1, keepdims=True)
    xc = xf - mu
    y = xc * lax.rsqrt(jnp.mean(xc * xc, axis=-1, keepdims=True) + NORM_EPS)
    return (y * g.astype(jnp.float32) + b.astype(jnp.float32)).astype(x.dtype)


def t5_causal_bucket(dist):
    n = np.maximum(dist, 0)
    max_exact = REL_BUCKETS // 2
    large = max_exact + (np.log(np.maximum(n, 1).astype(np.float32) / max_exact)
                         / math.log(REL_MAX_DIST / max_exact)
                         * (REL_BUCKETS - max_exact)).astype(np.int32)
    large = np.minimum(large, REL_BUCKETS - 1)
    return np.where(n < max_exact, n, large).astype(np.int32)


def conformer_conv(x, norm_g, w_in, b_in, dw, dw_b, ln_g, ln_b, w_out, b_out):
    h = rms_norm(x, norm_g)
    u = h @ w_in + b_in
    a, gate = jnp.split(u, 2, axis=-1)
    u = a * jax.nn.sigmoid(gate)
    u = lax.conv_general_dilated(
        u, dw[:, None, :].astype(u.dtype), window_strides=(1,),
        padding=[(CONV_WIDTH - 1, 0)],
        dimension_numbers=("NWC", "WIO", "NWC"),
        feature_group_count=D_MODEL) + dw_b
    u = jax.nn.silu(layer_norm(u, ln_g, ln_b))
    return u @ w_out + b_out


def swa_sink_attention(x, norm_g, w_qkv, b_qkv, q_g, k_g, sinks, w_o, b_o, rel_bias):
    B, S, _ = x.shape
    nb = S // BLOCK
    h = rms_norm(x, norm_g)
    qkv = h @ w_qkv + b_qkv
    q, k, v = jnp.split(qkv, [ATTN_DIM, ATTN_DIM + N_KV_HEADS * HEAD_DIM], axis=-1)
    q = rms_norm(q.reshape(B, S, N_HEADS, HEAD_DIM), q_g)
    k = rms_norm(k.reshape(B, S, N_KV_HEADS, HEAD_DIM), k_g)
    v = v.reshape(B, S, N_KV_HEADS, HEAD_DIM)

    qb = q.reshape(B, nb, BLOCK, N_KV_HEADS, GROUP, HEAD_DIM)

    def band(t):
        tp = jnp.pad(t, ((0, 0), (BLOCK, 0), (0, 0), (0, 0)))
        tb = tp.reshape(B, nb + 1, BLOCK, N_KV_HEADS, HEAD_DIM)
        return jnp.concatenate([tb[:, :-1], tb[:, 1:]], axis=2)

    kband, vband = band(k), band(v)

    q_loc = np.arange(BLOCK)[:, None]
    k_loc = np.arange(2 * BLOCK)[None, :]
    dist = q_loc + BLOCK - k_loc
    in_window = (dist >= 0) & (dist < WINDOW)
    k_abs = (np.arange(nb)[:, None, None] - 1) * BLOCK + k_loc[None]
    mask = jnp.asarray(in_window[None] & (k_abs >= 0))
    bias = rel_bias[jnp.asarray(t5_causal_bucket(dist))]
    bias = jnp.transpose(bias, (2, 0, 1)).reshape(N_KV_HEADS, GROUP, BLOCK, 2 * BLOCK)

    scale = 1.0 / math.sqrt(HEAD_DIM)
    logits = jnp.einsum("bnqkgd,bnskd->bnkgqs", qb, kband).astype(jnp.float32) * scale
    logits = logits + bias.astype(jnp.float32)[None, None]
    logits = jnp.where(mask[None, :, None, None], logits, NEG_INF)
    sink_col = jnp.broadcast_to(
        sinks.astype(jnp.float32).reshape(1, 1, N_KV_HEADS, GROUP, 1, 1),
        logits.shape[:-1] + (1,))
    probs = jax.nn.softmax(jnp.concatenate([logits, sink_col], axis=-1), axis=-1)[..., :-1]
    out = jnp.einsum("bnkgqs,bnskd->bnqkgd", probs.astype(vband.dtype), vband)
    out = out.reshape(B, S, ATTN_DIM)
    return out @ w_o + b_o


def sq_relu_mlp(x, norm_g, w_up, w_down):
    h = rms_norm(x, norm_g)
    return jnp.square(jax.nn.relu(h @ w_up)) @ w_down


def setup_inputs(seed: int = 0) -> dict:
    key = jax.random.key(seed)
    ks = iter(jax.random.split(key, 32))
    f32 = jnp.float32

    def nrm(shape, scale):
        return jax.random.normal(next(ks), shape, f32) * scale

    NC, NA = N_CONV_LAYERS, N_ATTN_LAYERS
    return {
        "x": nrm((BATCH, SEQ, D_MODEL), 1.0),
        "conv_norm_g": 1.0 + nrm((NC, D_MODEL), 0.02),
        "conv_w_in": nrm((NC, D_MODEL, 2 * D_MODEL), D_MODEL ** -0.5),
        "conv_b_in": nrm((NC, 2 * D_MODEL), 0.02),
        "conv_dw": nrm((NC, CONV_WIDTH, D_MODEL), CONV_WIDTH ** -0.5),
        "conv_dw_b": nrm((NC, D_MODEL), 0.02),
        "conv_ln_g": 1.0 + nrm((NC, D_MODEL), 0.02),
        "conv_ln_b": nrm((NC, D_MODEL), 0.02),
        "conv_w_out": nrm((NC, D_MODEL, D_MODEL), D_MODEL ** -0.5),
        "conv_b_out": nrm((NC, D_MODEL), 0.02),
        "attn_norm_g": 1.0 + nrm((NA, D_MODEL), 0.02),
        "w_qkv": nrm((NA, D_MODEL, QKV_DIM), D_MODEL ** -0.5),
        "b_qkv": nrm((NA, QKV_DIM), 0.02),
        "q_norm_g": 1.0 + nrm((NA, HEAD_DIM), 0.02),
        "k_norm_g": 1.0 + nrm((NA, HEAD_DIM), 0.02),
        "sinks": nrm((NA, N_HEADS), 0.5),
        "w_o": nrm((NA, ATTN_DIM, D_MODEL), ATTN_DIM ** -0.5),
        "b_o": nrm((NA, D_MODEL), 0.02),
        "rel_bias": nrm((REL_BUCKETS, N_HEADS), 0.5),
        "mlp_norm_g": 1.0 + nrm((DEPTH, D_MODEL), 0.02),
        "w_up": nrm((DEPTH, D_MODEL, D_FF), D_MODEL ** -0.5),
        "w_down": nrm((DEPTH, D_FF, D_MODEL), D_FF ** -0.5),
    }


def reference(x, conv_norm_g, conv_w_in, conv_b_in, conv_dw, conv_dw_b, conv_ln_g,
              conv_ln_b, conv_w_out, conv_b_out, attn_norm_g, w_qkv, b_qkv, q_norm_g,
              k_norm_g, sinks, w_o, b_o, rel_bias, mlp_norm_g, w_up, w_down):
    for i in range(DEPTH):
        j = i // N_MIXERS
        if i % N_MIXERS == 0:
            x = x + conformer_conv(x, conv_norm_g[j], conv_w_in[j], conv_b_in[j],
                                   conv_dw[j], conv_dw_b[j], conv_ln_g[j], conv_ln_b[j],
                                   conv_w_out[j], conv_b_out[j])
        else:
            x = x + swa_sink_attention(x, attn_norm_g[j], w_qkv[j], b_qkv[j],
                                       q_norm_g[j], k_norm_g[j], sinks[j], w_o[j],
                                       b_o[j], rel_bias)
        x = x + sq_relu_mlp(x, mlp_norm_g[i], w_up[i], w_down[i])
    return x
```

```python
import functools
import math

import numpy as np
import jax
import jax.numpy as jnp
from jax import lax
from jax.experimental import pallas as pl
from jax.experimental.pallas import tpu as pltpu

F32 = jnp.float32
BF16 = jnp.bfloat16

CONV_WIDTH = 31
N_HEADS = 16
N_KV_HEADS = 2
HEAD_DIM = 64
GROUP = N_HEADS // N_KV_HEADS
WINDOW = 128
REL_BUCKETS = 32
REL_MAX_DIST = 128
NORM_EPS = 1e-6
NEG_INF = -1e30

SUBLANES = 8
LANES = 128
VMEM_LIMIT_BYTES = 56 * 1024 * 1024

TOKEN_TILE = 512
FF_CHUNK = 1024
CONV_HALO = 32
CONV_ROWS = 64


def _rms(x, g):
    return x * lax.rsqrt(jnp.mean(x * x, axis=-1, keepdims=True) + NORM_EPS) * g


def _sigmoid(x):
    return 1.0 / (1.0 + jnp.exp(-x))


def _const_spec(shape):
    n = len(shape)
    return pl.BlockSpec(shape, lambda i: (0,) * n, pipeline_mode=pl.Buffered(1))


def _compiler_params():
    return pltpu.CompilerParams(dimension_semantics=("arbitrary",),
                                vmem_limit_bytes=VMEM_LIMIT_BYTES)


def _mlp_body(x_ref, g_ref, wu_ref, wd_ref, o_ref):
    x = x_ref[...]
    h = _rms(x, g_ref[...]).astype(BF16)
    o_ref[...] = x

    def chunk(c, carry):
        hid = jnp.dot(h, wu_ref[c], preferred_element_type=F32)
        hid = jnp.square(jnp.maximum(hid, 0.0)).astype(BF16)
        o_ref[...] += jnp.dot(hid, wd_ref[c], preferred_element_type=F32)
        return carry

    lax.fori_loop(0, wu_ref.shape[0], chunk, 0)


def _mlp(x, g, w_up, w_down):
    t, d = x.shape
    f = w_up.shape[1]
    nc = f // FF_CHUNK
    wu = w_up.astype(BF16).reshape(d, nc, FF_CHUNK).transpose(1, 0, 2)
    wd = w_down.astype(BF16).reshape(nc, FF_CHUNK, d)
    return pl.pallas_call(
        _mlp_body,
        grid=(t // TOKEN_TILE,),
        in_specs=[
            pl.BlockSpec((TOKEN_TILE, d), lambda i: (i, 0)),
            _const_spec((1, d)),
            _const_spec((nc, d, FF_CHUNK)),
            _const_spec((nc, FF_CHUNK, d)),
        ],
        out_specs=pl.BlockSpec((TOKEN_TILE, d), lambda i: (i, 0)),
        out_shape=jax.ShapeDtypeStruct((t, d), F32),
        compiler_params=_compiler_params(),
        name="mlp_block",
    )(x, g.reshape(1, d), wu, wd)


def _conv_body(x_ref, ng_ref, win_ref, bin_ref, dw_ref, dwb_ref, lng_ref, lnb_ref,
               wout_ref, bout_ref, o_ref, ubuf, cbuf, *, tiles_per_seq):
    tm, d = x_ref.shape
    first = (pl.program_id(0) % tiles_per_seq) == 0

    @pl.when(first)
    def _():
        ubuf[0:CONV_HALO, :] = jnp.zeros((CONV_HALO, d), F32)

    @pl.when(jnp.logical_not(first))
    def _():
        ubuf[0:CONV_HALO, :] = ubuf[tm:tm + CONV_HALO, :]

    x = x_ref[...]
    h = _rms(x, ng_ref[...]).astype(BF16)
    u = jnp.dot(h, win_ref[...], preferred_element_type=F32) + bin_ref[...]
    ubuf[CONV_HALO:CONV_HALO + tm, :] = u[:, :d] * _sigmoid(u[:, d:])

    lead = CONV_HALO - (CONV_WIDTH - 1)
    win_rows = CONV_ROWS + CONV_HALO

    def rows(r, carry):
        r0 = pl.multiple_of(r * CONV_ROWS, CONV_ROWS)
        w = ubuf[pl.ds(r0, win_rows), :]
        acc = jnp.zeros((CONV_ROWS, d), F32) + dwb_ref[...]
        for b in range(SUBLANES):
            wb = w if b == 0 else pltpu.roll(w, win_rows - b, axis=0)
            for a in range((CONV_HALO + SUBLANES) // SUBLANES):
                k = a * SUBLANES + b - lead
                if 0 <= k < CONV_WIDTH:
                    acc = acc + wb[a * SUBLANES:a * SUBLANES + CONV_ROWS, :] * dw_ref[k:k + 1, :]
        cbuf[pl.ds(r0, CONV_ROWS), :] = acc
        return carry

    lax.fori_loop(0, tm // CONV_ROWS, rows, 0)

    c = cbuf[...]
    mu = jnp.mean(c, axis=-1, keepdims=True)
    cc = c - mu
    y = cc * lax.rsqrt(jnp.mean(cc * cc, axis=-1, keepdims=True) + NORM_EPS)
    y = y * lng_ref[...] + lnb_ref[...]
    y = (y * _sigmoid(y)).astype(BF16)
    o_ref[...] = x + jnp.dot(y, wout_ref[...], preferred_element_type=F32) + bout_ref[...]


def _conv_block(x, seq, ng, w_in, b_in, dw, dw_b, ln_g, ln_b, w_out, b_out):
    t, d = x.shape
    row = lambda v: v.reshape(1, -1)
    body = functools.partial(_conv_body, tiles_per_seq=seq // TOKEN_TILE)
    return pl.pallas_call(
        body,
        grid=(t // TOKEN_TILE,),
        in_specs=[
            pl.BlockSpec((TOKEN_TILE, d), lambda i: (i, 0)),
            _const_spec((1, d)),
            _const_spec((d, 2 * d)),
            _const_spec((1, 2 * d)),
            _const_spec((CONV_WIDTH, d)),
            _const_spec((1, d)),
            _const_spec((1, d)),
            _const_spec((1, d)),
            _const_spec((d, d)),
            _const_spec((1, d)),
        ],
        out_specs=pl.BlockSpec((TOKEN_TILE, d), lambda i: (i, 0)),
        out_shape=jax.ShapeDtypeStruct((t, d), F32),
        scratch_shapes=[
            pltpu.VMEM((TOKEN_TILE + CONV_HALO, d), F32),
            pltpu.VMEM((TOKEN_TILE, d), F32),
        ],
        compiler_params=_compiler_params(),
        name="conv_block",
    )(x, row(ng), w_in.astype(BF16), row(b_in), dw, row(dw_b), row(ln_g), row(ln_b),
      w_out.astype(BF16), row(b_out))


def _t5_causal_bucket(dist):
    n = np.maximum(dist, 0)
    max_exact = REL_BUCKETS // 2
    large = max_exact + (np.log(np.maximum(n, 1).astype(np.float32) / max_exact)
                         / math.log(REL_MAX_DIST / max_exact)
                         * (REL_BUCKETS - max_exact)).astype(np.int32)
    large = np.minimum(large, REL_BUCKETS - 1)
    return np.where(n < max_exact, n, large).astype(np.int32)


def _bias_body(rel_ref, bucket_ref, o_ref):
    hd = pl.program_id(0)
    bucket = bucket_ref[...]
    acc = jnp.zeros(bucket.shape, F32)
    for b in range(REL_BUCKETS):
        acc = jnp.where(bucket == b, rel_ref[b, hd], acc)
    o_ref[0] = acc


def _band_bias(rel_bias):
    q_loc = np.arange(WINDOW)[:, None]
    k_loc = np.arange(2 * WINDOW)[None, :]
    bucket = jnp.asarray(_t5_causal_bucket(q_loc + WINDOW - k_loc))
    return pl.pallas_call(
        _bias_body,
        grid=(N_HEADS,),
        in_specs=[
            pl.BlockSpec(memory_space=pltpu.SMEM),
            pl.BlockSpec((WINDOW, 2 * WINDOW), lambda i: (0, 0)),
        ],
        out_specs=pl.BlockSpec((1, WINDOW, 2 * WINDOW), lambda i: (i, 0, 0)),
        out_shape=jax.ShapeDtypeStruct((N_HEADS, WINDOW, 2 * WINDOW), F32),
        name="band_bias",
    )(rel_bias, bucket)


def _attn_body(x_ref, ng_ref, wqkv_ref, bqkv_ref, qg_ref, kg_ref, gmat_ref, bias_ref,
               sinks_ref, wo_ref, bo_ref, o_ref, kbuf, vbuf, obuf, *, tiles_per_seq):
    tq, d = x_ref.shape
    kvd = N_KV_HEADS * HEAD_DIM
    first = (pl.program_id(0) % tiles_per_seq) == 0

    @pl.when(first)
    def _():
        kbuf[0:WINDOW, :] = jnp.zeros((WINDOW, kvd), BF16)
        vbuf[0:WINDOW, :] = jnp.zeros((WINDOW, kvd), BF16)

    @pl.when(jnp.logical_not(first))
    def _():
        kbuf[0:WINDOW, :] = kbuf[tq:tq + WINDOW, :]
        vbuf[0:WINDOW, :] = vbuf[tq:tq + WINDOW, :]

    x = x_ref[...]
    h = _rms(x, ng_ref[...]).astype(BF16)
    qkv = jnp.dot(h, wqkv_ref[...], preferred_element_type=F32) + bqkv_ref[...]
    q = qkv[:, :d]
    k = qkv[:, d:d + kvd]
    v = qkv[:, d + kvd:]
    gmat = gmat_ref[...]
    qms = jnp.dot((q * q).astype(BF16), gmat, preferred_element_type=F32)
    kms = jnp.dot((k * k).astype(BF16), gmat[:kvd, :kvd], preferred_element_type=F32)
    scale = 1.0 / math.sqrt(HEAD_DIM)
    qn = (q * lax.rsqrt(qms + NORM_EPS) * (qg_ref[...] * scale)).astype(BF16)
    kbuf[WINDOW:WINDOW + tq, :] = (k * lax.rsqrt(kms + NORM_EPS) * kg_ref[...]).astype(BF16)
    vbuf[WINDOW:WINDOW + tq, :] = v.astype(BF16)

    qq = lax.broadcasted_iota(jnp.int32, (WINDOW, 2 * WINDOW), 0)
    kk = lax.broadcasted_iota(jnp.int32, (WINDOW, 2 * WINDOW), 1)
    in_window = (kk > qq) & (kk <= qq + WINDOW)
    first_key = jnp.where(first, WINDOW, 0)

    for j in range(tq // WINDOW):
        mask = in_window
        if j == 0:
            mask = in_window & (kk >= first_key)
        qj = qn[j * WINDOW:(j + 1) * WINDOW, :]
        kj = kbuf[j * WINDOW:(j + 2) * WINDOW, :]
        vj = vbuf[j * WINDOW:(j + 2) * WINDOW, :]
        for g in range(N_KV_HEADS):
            kg = kj[:, g * HEAD_DIM:(g + 1) * HEAD_DIM]
            vg = vj[:, g * HEAD_DIM:(g + 1) * HEAD_DIM]
            qs = jnp.concatenate(
                [qj[:, (g * GROUP + hh) * HEAD_DIM:(g * GROUP + hh + 1) * HEAD_DIM]
                 for hh in range(GROUP)], axis=0)
            s = lax.dot_general(qs, kg, (((1,), (1,)), ((), ())),
                                preferred_element_type=F32)
            ps = []
            inv = []
            for hh in range(GROUP):
                hd = g * GROUP + hh
                sh = s[hh * WINDOW:(hh + 1) * WINDOW, :] + bias_ref[hd]
                sh = jnp.where(mask, sh, NEG_INF)
                sink = sinks_ref[hd]
                m = jnp.maximum(jnp.max(sh, axis=-1, keepdims=True), sink)
                p = jnp.exp(sh - m)
                denom = jnp.sum(p, axis=-1, keepdims=True) + jnp.exp(sink - m)
                ps.append(p.astype(BF16))
                inv.append(1.0 / denom)
            o = jnp.dot(jnp.concatenate(ps, axis=0), vg, preferred_element_type=F32)
            for hh in range(GROUP):
                hd = g * GROUP + hh
                obuf[j * WINDOW:(j + 1) * WINDOW, hd * HEAD_DIM:(hd + 1) * HEAD_DIM] = (
                    o[hh * WINDOW:(hh + 1) * WINDOW, :] * inv[hh])

    att = obuf[...].astype(BF16)
    o_ref[...] = x + jnp.dot(att, wo_ref[...], preferred_element_type=F32) + bo_ref[...]


def _attn_block(x, seq, ng, w_qkv, b_qkv, q_g, k_g, sinks, w_o, b_o, bias):
    t, d = x.shape
    qkv_dim = w_qkv.shape[1]
    kvd = N_KV_HEADS * HEAD_DIM
    row = lambda v: v.reshape(1, -1)
    head_of_lane = np.arange(d) // HEAD_DIM
    gmat = jnp.asarray((head_of_lane[:, None] == head_of_lane[None, :]) / HEAD_DIM, BF16)
    body = functools.partial(_attn_body, tiles_per_seq=seq // TOKEN_TILE)
    return pl.pallas_call(
        body,
        grid=(t // TOKEN_TILE,),
        in_specs=[
            pl.BlockSpec((TOKEN_TILE, d), lambda i: (i, 0)),
            _const_spec((1, d)),
            _const_spec((d, qkv_dim)),
            _const_spec((1, qkv_dim)),
            _const_spec((1, d)),
            _const_spec((1, kvd)),
            _const_spec((d, d)),
            _const_spec((N_HEADS, WINDOW, 2 * WINDOW)),
            pl.BlockSpec(memory_space=pltpu.SMEM),
            _const_spec((d, d)),
            _const_spec((1, d)),
        ],
        out_specs=pl.BlockSpec((TOKEN_TILE, d), lambda i: (i, 0)),
        out_shape=jax.ShapeDtypeStruct((t, d), F32),
        scratch_shapes=[
            pltpu.VMEM((TOKEN_TILE + WINDOW, kvd), BF16),
            pltpu.VMEM((TOKEN_TILE + WINDOW, kvd), BF16),
            pltpu.VMEM((TOKEN_TILE, d), F32),
        ],
        compiler_params=_compiler_params(),
        name="attn_block",
    )(x, row(ng), w_qkv.astype(BF16), row(b_qkv), row(jnp.tile(q_g, N_HEADS)),
      row(jnp.tile(k_g, N_KV_HEADS)), gmat, bias, sinks, w_o.astype(BF16), row(b_o))


def kernel(x, conv_norm_g, conv_w_in, conv_b_in, conv_dw, conv_dw_b, conv_ln_g, conv_ln_b, conv_w_out, conv_b_out, attn_norm_g, w_qkv, b_qkv, q_norm_g, k_norm_g, sinks, w_o, b_o, rel_bias, mlp_norm_g, w_up, w_down):
    b, s, d = x.shape
    assert s % TOKEN_TILE == 0 and TOKEN_TILE % WINDOW == 0 and d == N_HEADS * HEAD_DIM
    depth = mlp_norm_g.shape[0]
    bias = _band_bias(rel_bias)
    xf = x.reshape(b * s, d)
    for i in range(depth):
        j = i // 2
        if i % 2 == 0:
            xf = _conv_block(xf, s, conv_norm_g[j], conv_w_in[j], conv_b_in[j], conv_dw[j],
                             conv_dw_b[j], conv_ln_g[j], conv_ln_b[j], conv_w_out[j],
                             conv_b_out[j])
        else:
            xf = _attn_block(xf, s, attn_norm_g[j], w_qkv[j], b_qkv[j], q_norm_g[j],
                             k_norm_g[j], sinks[j], w_o[j], b_o[j], bias)
        xf = _mlp(xf, mlp_norm_g[i], w_up[i], w_down[i])
    return xf.reshape(b, s, d)
```

```python
import functools
import math

import numpy as np
import jax
import jax.numpy as jnp
from jax import lax
from jax.experimental import pallas as pl
from jax.experimental.pallas import tpu as pltpu

F32 = jnp.float32
BF16 = jnp.bfloat16

CONV_WIDTH = 31
N_HEADS = 16
N_KV_HEADS = 2
HEAD_DIM = 64
GROUP = N_HEADS // N_KV_HEADS
WINDOW = 128
REL_BUCKETS = 32
REL_MAX_DIST = 128
NORM_EPS = 1e-6
NEG_INF = -1e30

SUBLANES = 8
LANES = 128
BF16_ROWS = 16
VMEM_LIMIT_BYTES = 56 * 1024 * 1024

TOKEN_TILE = 512
FF_CHUNK = 1024
CONV_HALO = 32
CONV_ROWS = 64


def _rms(x, g):
    return x * lax.rsqrt(jnp.mean(x * x, axis=-1, keepdims=True) + NORM_EPS) * g


def _sigmoid(x):
    return 1.0 / (1.0 + jnp.exp(-x))


def _const_spec(shape):
    n = len(shape)
    return pl.BlockSpec(shape, lambda i: (0,) * n, pipeline_mode=pl.Buffered(1))


def _compiler_params():
    return pltpu.CompilerParams(dimension_semantics=("arbitrary",),
                                vmem_limit_bytes=VMEM_LIMIT_BYTES)


def _mlp_body(x_ref, g_ref, wu_ref, wd_ref, o_ref):
    x = x_ref[...]
    h = _rms(x, g_ref[...]).astype(BF16)
    acc = x
    for c in range(wu_ref.shape[1] // FF_CHUNK):
        cols = slice(c * FF_CHUNK, (c + 1) * FF_CHUNK)
        hid = jnp.dot(h, wu_ref[:, cols], preferred_element_type=F32)
        hid = jnp.square(jnp.maximum(hid, 0.0)).astype(BF16)
        acc = acc + jnp.dot(hid, wd_ref[cols, :], preferred_element_type=F32)
    o_ref[...] = acc


def _mlp(x, g, w_up, w_down):
    t, d = x.shape
    f = w_up.shape[1]
    return pl.pallas_call(
        _mlp_body,
        grid=(t // TOKEN_TILE,),
        in_specs=[
            pl.BlockSpec((TOKEN_TILE, d), lambda i: (i, 0)),
            _const_spec((1, d)),
            _const_spec((d, f)),
            _const_spec((f, d)),
        ],
        out_specs=pl.BlockSpec((TOKEN_TILE, d), lambda i: (i, 0)),
        out_shape=jax.ShapeDtypeStruct((t, d), F32),
        compiler_params=_compiler_params(),
        name="mlp_block",
    )(x, g.reshape(1, d), w_up.astype(BF16), w_down.astype(BF16))


def _conv_body(x_ref, ng_ref, win_ref, bin_ref, dw_ref, dwb_ref, lng_ref, lnb_ref,
               wout_ref, bout_ref, o_ref, ubuf, cbuf, *, tiles_per_seq):
    tm, d = x_ref.shape
    first = (pl.program_id(0) % tiles_per_seq) == 0

    @pl.when(first)
    def _():
        ubuf[0:CONV_HALO, :] = jnp.zeros((CONV_HALO, d), F32)

    @pl.when(jnp.logical_not(first))
    def _():
        ubuf[0:CONV_HALO, :] = ubuf[tm:tm + CONV_HALO, :]

    x = x_ref[...]
    h = _rms(x, ng_ref[...]).astype(BF16)
    u = jnp.dot(h, win_ref[...], preferred_element_type=F32) + bin_ref[...]
    ubuf[CONV_HALO:CONV_HALO + tm, :] = u[:, :d] * _sigmoid(u[:, d:])

    lead = CONV_HALO - (CONV_WIDTH - 1)
    win_rows = CONV_ROWS + CONV_HALO

    def rows(r, carry):
        r0 = pl.multiple_of(r * CONV_ROWS, CONV_ROWS)
        w = ubuf[pl.ds(r0, win_rows), :]
        acc = jnp.zeros((CONV_ROWS, d), F32) + dwb_ref[...]
        for b in range(SUBLANES):
            wb = w if b == 0 else pltpu.roll(w, win_rows - b, axis=0)
            for a in range((CONV_HALO + SUBLANES) // SUBLANES):
                k = a * SUBLANES + b - lead
                if 0 <= k < CONV_WIDTH:
                    acc = acc + wb[a * SUBLANES:a * SUBLANES + CONV_ROWS, :] * dw_ref[k:k + 1, :]
        cbuf[pl.ds(r0, CONV_ROWS), :] = acc
        return carry

    lax.fori_loop(0, tm // CONV_ROWS, rows, 0)

    c = cbuf[...]
    mu = jnp.mean(c, axis=-1, keepdims=True)
    cc = c - mu
    y = cc * lax.rsqrt(jnp.mean(cc * cc, axis=-1, keepdims=True) + NORM_EPS)
    y = y * lng_ref[...] + lnb_ref[...]
    y = (y * _sigmoid(y)).astype(BF16)
    o_ref[...] = x + jnp.dot(y, wout_ref[...], preferred_element_type=F32) + bout_ref[...]


def _conv_block(x, seq, ng, w_in, b_in, dw, dw_b, ln_g, ln_b, w_out, b_out):
    t, d = x.shape
    row = lambda v: v.reshape(1, -1)
    body = functools.partial(_conv_body, tiles_per_seq=seq // TOKEN_TILE)
    return pl.pallas_call(
        body,
        grid=(t // TOKEN_TILE,),
        in_specs=[
            pl.BlockSpec((TOKEN_TILE, d), lambda i: (i, 0)),
            _const_spec((1, d)),
            _const_spec((d, 2 * d)),
            _const_spec((1, 2 * d)),
            _const_spec((CONV_WIDTH, d)),
            _const_spec((1, d)),
            _const_spec((1, d)),
            _const_spec((1, d)),
            _const_spec((d, d)),
            _const_spec((1, d)),
        ],
        out_specs=pl.BlockSpec((TOKEN_TILE, d), lambda i: (i, 0)),
        out_shape=jax.ShapeDtypeStruct((t, d), F32),
        scratch_shapes=[
            pltpu.VMEM((TOKEN_TILE + CONV_HALO, d), F32),
            pltpu.VMEM((TOKEN_TILE, d), F32),
        ],
        compiler_params=_compiler_params(),
        name="conv_block",
    )(x, row(ng), w_in.astype(BF16), row(b_in), dw, row(dw_b), row(ln_g), row(ln_b),
      w_out.astype(BF16), row(b_out))


def _t5_causal_bucket(dist):
    n = np.maximum(dist, 0)
    max_exact = REL_BUCKETS // 2
    large = max_exact + (np.log(np.maximum(n, 1).astype(np.float32) / max_exact)
                         / math.log(REL_MAX_DIST / max_exact)
                         * (REL_BUCKETS - max_exact)).astype(np.int32)
    large = np.minimum(large, REL_BUCKETS - 1)
    return np.where(n < max_exact, n, large).astype(np.int32)


def _bias_body(rel_ref, bucket_ref, o_ref):
    hd = pl.program_id(0)
    bucket = bucket_ref[...]
    acc = jnp.zeros(bucket.shape, F32)
    for b in range(REL_BUCKETS):
        acc = jnp.where(bucket == b, rel_ref[b, hd], acc)
    o_ref[0] = acc


def _band_bias(rel_bias):
    k_loc = np.arange(2 * WINDOW)[:, None]
    q_loc = np.arange(WINDOW)[None, :]
    bucket = jnp.asarray(_t5_causal_bucket(q_loc + WINDOW - k_loc))
    return pl.pallas_call(
        _bias_body,
        grid=(N_HEADS,),
        in_specs=[
            pl.BlockSpec(memory_space=pltpu.SMEM),
            pl.BlockSpec((2 * WINDOW, WINDOW), lambda i: (0, 0)),
        ],
        out_specs=pl.BlockSpec((1, 2 * WINDOW, WINDOW), lambda i: (i // GROUP, 0, i % GROUP)),
        out_shape=jax.ShapeDtypeStruct((N_KV_HEADS, 2 * WINDOW, GROUP * WINDOW), F32),
        name="band_bias",
    )(rel_bias, bucket)


def _attn_body(x_ref, ng_ref, wqkv_ref, bqkv_ref, kgain_ref, bias_ref, sinks_ref, wo_ref,
               bo_ref, o_ref, qbuf, kbuf, vbuf, abuf, *, tiles_per_seq):
    tq, d = x_ref.shape
    kvd = N_KV_HEADS * HEAD_DIM
    nblk = tq // WINDOW
    first = (pl.program_id(0) % tiles_per_seq) == 0

    @pl.when(first)
    def _():
        kbuf[0:WINDOW, :] = jnp.zeros((WINDOW, kvd), BF16)
        vbuf[:, 0:WINDOW] = jnp.zeros((kvd, WINDOW), BF16)

    @pl.when(jnp.logical_not(first))
    def _():
        kbuf[0:WINDOW, :] = kbuf[tq:tq + WINDOW, :]
        vbuf[:, 0:WINDOW] = vbuf[:, tq:tq + WINDOW]

    x = x_ref[...]
    h = _rms(x, ng_ref[...]).astype(BF16)
    qkv = lax.dot_general(wqkv_ref[...], h, (((1,), (1,)), ((), ())),
                          preferred_element_type=F32)
    bcol = bqkv_ref[...]

    def head_rows(r0):
        rows = slice(r0, r0 + HEAD_DIM)
        return qkv[rows, :] + jnp.concatenate([bcol[rows, :]] * nblk, axis=1)

    for hd in range(N_HEADS):
        qh = head_rows(hd * HEAD_DIM)
        ms = jnp.mean(qh * qh, axis=0, keepdims=True)
        qbuf[hd * HEAD_DIM:(hd + 1) * HEAD_DIM, :] = (qh * lax.rsqrt(ms + NORM_EPS)).astype(BF16)
    kn = []
    for g in range(N_KV_HEADS):
        kh = head_rows(d + g * HEAD_DIM)
        ms = jnp.mean(kh * kh, axis=0, keepdims=True)
        kn.append(kh * lax.rsqrt(ms + NORM_EPS))
        vbuf[g * HEAD_DIM:(g + 1) * HEAD_DIM, WINDOW:WINDOW + tq] = (
            head_rows(d + kvd + g * HEAD_DIM).astype(BF16))
    kt = jnp.concatenate(kn, axis=0)
    kbuf[WINDOW:WINDOW + tq, :] = (kt.T * kgain_ref[...]).astype(BF16)

    kk = lax.broadcasted_iota(jnp.int32, (2 * WINDOW, WINDOW), 0)
    qq = lax.broadcasted_iota(jnp.int32, (2 * WINDOW, WINDOW), 1)
    in_window = (kk > qq) & (kk <= qq + WINDOW)
    first_key = jnp.where(first, WINDOW, 0)
    ones_rows = (lax.broadcasted_iota(jnp.int32, (BF16_ROWS, 2 * WINDOW), 0) == 0).astype(BF16)

    for j in range(nblk):
        mask = in_window
        if j == 0:
            mask = in_window & (kk >= first_key)
        cols = slice(j * WINDOW, (j + 1) * WINDOW)
        band = slice(j * WINDOW, (j + 2) * WINDOW)
        for g in range(N_KV_HEADS):
            heads = range(g * GROUP, (g + 1) * GROUP)
            qt = jnp.concatenate(
                [qbuf[hd * HEAD_DIM:(hd + 1) * HEAD_DIM, cols] for hd in heads], axis=1)
            kw = kbuf[band, g * HEAD_DIM:(g + 1) * HEAD_DIM]
            st = jnp.dot(kw, qt, preferred_element_type=F32)
            ps = []
            ms = []
            for hh, hd in enumerate(heads):
                blk = slice(hh * WINDOW, (hh + 1) * WINDOW)
                s = jnp.where(mask, st[:, blk] + bias_ref[g, :, blk], NEG_INF)
                m = jnp.maximum(jnp.max(s, axis=0, keepdims=True), sinks_ref[hd])
                ps.append(jnp.exp(s - m).astype(BF16))
                ms.append(m)
            pt = jnp.concatenate(ps, axis=1)
            va = jnp.concatenate([vbuf[g * HEAD_DIM:(g + 1) * HEAD_DIM, band], ones_rows], axis=0)
            ot = jnp.dot(va, pt, preferred_element_type=F32)
            for hh, hd in enumerate(heads):
                blk = slice(hh * WINDOW, (hh + 1) * WINDOW)
                denom = ot[HEAD_DIM:HEAD_DIM + 1, blk] + jnp.exp(sinks_ref[hd] - ms[hh])
                abuf[hd * HEAD_DIM:(hd + 1) * HEAD_DIM, cols] = (
                    ot[0:HEAD_DIM, blk] * (1.0 / denom)).astype(BF16)

    yt = jnp.dot(wo_ref[...], abuf[...], preferred_element_type=F32)
    o_ref[...] = x + yt.T + bo_ref[...]


def _attn_block(x, seq, ng, w_qkv, b_qkv, q_g, k_g, sinks, w_o, b_o, bias):
    t, d = x.shape
    qkv_dim = w_qkv.shape[1]
    kvd = N_KV_HEADS * HEAD_DIM
    row = lambda v: v.reshape(1, -1)
    kgain = jnp.tile(q_g * k_g * (1.0 / math.sqrt(HEAD_DIM)), N_KV_HEADS)
    body = functools.partial(_attn_body, tiles_per_seq=seq // TOKEN_TILE)
    return pl.pallas_call(
        body,
        grid=(t // TOKEN_TILE,),
        in_specs=[
            pl.BlockSpec((TOKEN_TILE, d), lambda i: (i, 0)),
            _const_spec((1, d)),
            _const_spec((qkv_dim, d)),
            _const_spec((qkv_dim, LANES)),
            _const_spec((1, kvd)),
            _const_spec((N_KV_HEADS, 2 * WINDOW, GROUP * WINDOW)),
            pl.BlockSpec(memory_space=pltpu.SMEM),
            _const_spec((d, d)),
            _const_spec((1, d)),
        ],
        out_specs=pl.BlockSpec((TOKEN_TILE, d), lambda i: (i, 0)),
        out_shape=jax.ShapeDtypeStruct((t, d), F32),
        scratch_shapes=[
            pltpu.VMEM((d, TOKEN_TILE), BF16),
            pltpu.VMEM((TOKEN_TILE + WINDOW, kvd), BF16),
            pltpu.VMEM((kvd, TOKEN_TILE + WINDOW), BF16),
            pltpu.VMEM((d, TOKEN_TILE), BF16),
        ],
        compiler_params=_compiler_params(),
        name="attn_block",
    )(x, row(ng), w_qkv.T.astype(BF16), jnp.broadcast_to(b_qkv[:, None], (qkv_dim, LANES)),
      row(kgain), bias, sinks, w_o.T.astype(BF16), row(b_o))


def kernel(x, conv_norm_g, conv_w_in, conv_b_in, conv_dw, conv_dw_b, conv_ln_g, conv_ln_b, conv_w_out, conv_b_out, attn_norm_g, w_qkv, b_qkv, q_norm_g, k_norm_g, sinks, w_o, b_o, rel_bias, mlp_norm_g, w_up, w_down):
    b, s, d = x.shape
    assert s % TOKEN_TILE == 0 and TOKEN_TILE % WINDOW == 0 and d == N_HEADS * HEAD_DIM
    depth = mlp_norm_g.shape[0]
    bias = _band_bias(rel_bias)
    xf = x.reshape(b * s, d)
    for i in range(depth):
        j = i // 2
        if i % 2 == 0:
            xf = _conv_block(xf, s, conv_norm_g[j], conv_w_in[j], conv_b_in[j], conv_dw[j],
                             conv_dw_b[j], conv_ln_g[j], conv_ln_b[j], conv_w_out[j],
                             conv_b_out[j])
        else:
            xf = _attn_block(xf, s, attn_norm_g[j], w_qkv[j], b_qkv[j], q_norm_g[j],
                             k_norm_g[j], sinks[j], w_o[j], b_o[j], bias)
        xf = _mlp(xf, mlp_norm_g[i], w_up[i], w_down[i])
    return xf.reshape(b, s, d)
```

```python
import functools
import math

import numpy as np
import jax
import jax.numpy as jnp
from jax import lax
from jax.experimental import pallas as pl
from jax.experimental.pallas import tpu as pltpu

F32 = jnp.float32
BF16 = jnp.bfloat16

CONV_WIDTH = 31
N_HEADS = 16
N_KV_HEADS = 2
HEAD_DIM = 64
GROUP = N_HEADS // N_KV_HEADS
WINDOW = 128
REL_BUCKETS = 32
REL_MAX_DIST = 128
NORM_EPS = 1e-6
NEG_INF = -1e30
LOG2E = math.log2(math.e)

SUBLANES = 8
LANES = 128
BF16_ROWS = 16
MXU_COLS = 256
VMEM_LIMIT_BYTES = 56 * 1024 * 1024

TOKEN_TILE = 512
FF_CHUNK = 1024
CONV_HALO = 32
CONV_PHASES = 4
CONV_CHUNK = 32
CONV_TILE = 512
CONV_SPLIT = 2


def _rms(x, g):
    return x * lax.rsqrt(jnp.mean(x * x, axis=-1, keepdims=True) + NORM_EPS) * g


def _sigmoid(x):
    return 1.0 / (1.0 + jnp.exp(-x))


def _interleave(main, side):
    n, m, j = len(main), len(side), 0
    for i, step in enumerate(main):
        step()
        while j < m and (j + 1) * n <= (i + 1) * m:
            side[j]()
            j += 1
    for step in side[j:]:
        step()


def _const_spec(shape):
    n = len(shape)
    return pl.BlockSpec(shape, lambda i: (0,) * n, pipeline_mode=pl.Buffered(1))


def _compiler_params():
    return pltpu.CompilerParams(dimension_semantics=("arbitrary",),
                                vmem_limit_bytes=VMEM_LIMIT_BYTES)


def _mlp_body(x_ref, g_ref, wu_ref, wd_ref, o_ref):
    x = x_ref[...]
    h = _rms(x, g_ref[...]).astype(BF16)
    acc = x
    for c in range(wu_ref.shape[1] // FF_CHUNK):
        cols = slice(c * FF_CHUNK, (c + 1) * FF_CHUNK)
        hid = jnp.dot(h, wu_ref[:, cols], preferred_element_type=F32)
        hid = jnp.square(jnp.maximum(hid, 0.0)).astype(BF16)
        acc = acc + jnp.dot(hid, wd_ref[cols, :], preferred_element_type=F32)
    o_ref[...] = acc


def _mlp(x, g, w_up, w_down):
    t, d = x.shape
    f = w_up.shape[1]
    return pl.pallas_call(
        _mlp_body,
        grid=(t // TOKEN_TILE,),
        in_specs=[
            pl.BlockSpec((TOKEN_TILE, d), lambda i: (i, 0)),
            _const_spec((1, d)),
            _const_spec((d, f)),
            _const_spec((f, d)),
        ],
        out_specs=pl.BlockSpec((TOKEN_TILE, d), lambda i: (i, 0)),
        out_shape=jax.ShapeDtypeStruct((t, d), F32),
        compiler_params=_compiler_params(),
        name="mlp_block",
    )(x, g.reshape(1, d), w_up.astype(BF16), w_down.astype(BF16))


def _conv_body(x_ref, ng_ref, win_ref, bin_ref, dw_ref, dwb_ref, lng_ref, lnb_ref,
               wout_ref, bout_ref, o_ref, *bufs, tiles_per_seq):
    tm, d = x_ref.shape
    nslab = d // LANES
    part = tm // CONV_SPLIT
    span = CONV_PHASES * CONV_CHUNK
    first = (pl.program_id(0) % tiles_per_seq) == 0
    ubufs, cbufs = bufs[:CONV_SPLIT], bufs[CONV_SPLIT:]

    @pl.when(first)
    def _():
        ubufs[0][:, 0:CONV_HALO, :] = jnp.zeros((nslab, CONV_HALO, LANES), F32)

    @pl.when(jnp.logical_not(first))
    def _():
        ubufs[0][:, 0:CONV_HALO, :] = ubufs[-1][:, part:part + CONV_HALO, :]

    def glu_steps(gi):
        rows = slice(gi * part, (gi + 1) * part)
        h = _rms(x_ref[rows, :], ng_ref[...]).astype(BF16)

        def step(c0):
            def run():
                a_cols = slice(c0, c0 + MXU_COLS)
                g_cols = slice(d + c0, d + c0 + MXU_COLS)
                a = jnp.dot(h, win_ref[:, a_cols], preferred_element_type=F32) + bin_ref[:, a_cols]
                g = jnp.dot(h, win_ref[:, g_cols], preferred_element_type=F32) + bin_ref[:, g_cols]
                glu = a * _sigmoid(g)
                for i in range(MXU_COLS // LANES):
                    piece = glu[:, i * LANES:(i + 1) * LANES]
                    ubufs[gi][c0 // LANES + i, CONV_HALO:CONV_HALO + part, :] = piece
                    if gi + 1 < CONV_SPLIT:
                        ubufs[gi + 1][c0 // LANES + i, 0:CONV_HALO, :] = piece[part - CONV_HALO:, :]
            return run
        return [step(c0) for c0 in range(0, d, MXU_COLS)]

    def conv_steps(gi):
        def step(s):
            def run():
                lanes = slice(s * LANES, (s + 1) * LANES)
                for t0 in range(0, part, span):
                    acc = [jnp.zeros((CONV_CHUNK, LANES), F32) + dwb_ref[:, lanes]] * CONV_PHASES
                    for c in range(-(CONV_WIDTH - 1), CONV_PHASES):
                        lc = ubufs[gi][s, pl.ds(CONV_HALO + t0 + c, CONV_CHUNK, stride=CONV_PHASES), :]
                        for ph in range(CONV_PHASES):
                            k = c - ph + CONV_WIDTH - 1
                            if 0 <= k < CONV_WIDTH:
                                acc[ph] = acc[ph] + lc * dw_ref[k:k + 1, lanes]
                    for ph in range(CONV_PHASES):
                        cbufs[gi][s, pl.ds(t0 + ph, CONV_CHUNK, stride=CONV_PHASES), :] = acc[ph]
            return run
        return [step(s) for s in range(nslab)]

    def out_steps(gi):
        rows = slice(gi * part, (gi + 1) * part)
        c = jnp.concatenate([cbufs[gi][s] for s in range(nslab)], axis=1)
        mu = jnp.mean(c, axis=-1, keepdims=True)
        cc = c - mu
        y = cc * lax.rsqrt(jnp.mean(cc * cc, axis=-1, keepdims=True) + NORM_EPS)
        y = y * lng_ref[...] + lnb_ref[...]
        y = (y * _sigmoid(y)).astype(BF16)

        def step(c0):
            def run():
                cols = slice(c0, c0 + MXU_COLS)
                o_ref[rows, cols] = (x_ref[rows, cols] + bout_ref[:, cols]
                                     + jnp.dot(y, wout_ref[:, cols], preferred_element_type=F32))
            return run
        return [step(c0) for c0 in range(0, d, MXU_COLS)]

    _interleave(glu_steps(0), [])
    for gi in range(CONV_SPLIT):
        mxu_side = []
        if gi > 0:
            mxu_side += out_steps(gi - 1)
        if gi + 1 < CONV_SPLIT:
            mxu_side += glu_steps(gi + 1)
        _interleave(conv_steps(gi), mxu_side)
    _interleave(out_steps(CONV_SPLIT - 1), [])


def _conv_block(x, seq, ng, w_in, b_in, dw, dw_b, ln_g, ln_b, w_out, b_out):
    t, d = x.shape
    row = lambda v: v.reshape(1, -1)
    body = functools.partial(_conv_body, tiles_per_seq=seq // CONV_TILE)
    part = CONV_TILE // CONV_SPLIT
    return pl.pallas_call(
        body,
        grid=(t // CONV_TILE,),
        in_specs=[
            pl.BlockSpec((CONV_TILE, d), lambda i: (i, 0)),
            _const_spec((1, d)),
            _const_spec((d, 2 * d)),
            _const_spec((1, 2 * d)),
            _const_spec((CONV_WIDTH, d)),
            _const_spec((1, d)),
            _const_spec((1, d)),
            _const_spec((1, d)),
            _const_spec((d, d)),
            _const_spec((1, d)),
        ],
        out_specs=pl.BlockSpec((CONV_TILE, d), lambda i: (i, 0)),
        out_shape=jax.ShapeDtypeStruct((t, d), F32),
        scratch_shapes=(
            [pltpu.VMEM((d // LANES, part + CONV_HALO, LANES), F32)] * CONV_SPLIT
            + [pltpu.VMEM((d // LANES, part, LANES), F32)] * CONV_SPLIT),
        compiler_params=_compiler_params(),
        name="conv_block",
    )(x, row(ng), w_in.astype(BF16), row(b_in), dw, row(dw_b), row(ln_g), row(ln_b),
      w_out.astype(BF16), row(b_out))


def _t5_causal_bucket(dist):
    n = np.maximum(dist, 0)
    max_exact = REL_BUCKETS // 2
    large = max_exact + (np.log(np.maximum(n, 1).astype(np.float32) / max_exact)
                         / math.log(REL_MAX_DIST / max_exact)
                         * (REL_BUCKETS - max_exact)).astype(np.int32)
    large = np.minimum(large, REL_BUCKETS - 1)
    return np.where(n < max_exact, n, large).astype(np.int32)


def _bias_body(rel_ref, bucket_ref, o_ref):
    g = pl.program_id(0)
    bucket = bucket_ref[...]
    for hh in range(GROUP):
        acc = jnp.full(bucket.shape, NEG_INF, F32)
        for b in range(REL_BUCKETS):
            acc = jnp.where(bucket == b, rel_ref[b, g * GROUP + hh] * LOG2E, acc)
        o_ref[0, :, hh * WINDOW:(hh + 1) * WINDOW] = acc


def _band_bias(rel_bias):
    k_loc = np.arange(2 * WINDOW)[:, None]
    q_loc = np.arange(WINDOW)[None, :]
    dist = q_loc + WINDOW - k_loc
    bucket = np.where((dist >= 0) & (dist < WINDOW), _t5_causal_bucket(dist), -1)
    return pl.pallas_call(
        _bias_body,
        grid=(N_KV_HEADS,),
        in_specs=[
            pl.BlockSpec(memory_space=pltpu.SMEM),
            pl.BlockSpec((2 * WINDOW, WINDOW), lambda i: (0, 0)),
        ],
        out_specs=pl.BlockSpec((1, 2 * WINDOW, GROUP * WINDOW), lambda i: (i, 0, 0)),
        out_shape=jax.ShapeDtypeStruct((N_KV_HEADS, 2 * WINDOW, GROUP * WINDOW), F32),
        name="band_bias",
    )(rel_bias, jnp.asarray(bucket.astype(np.int32)))


def _attn_body(x_ref, ng_ref, wqkv_ref, bqkv_ref, kgain_ref, bias_ref, sinks_ref, wo_ref,
               bo_ref, o_ref, qbuf, kbuf, vbuf, abuf, *sbufs, tiles_per_seq):
    tq, d = x_ref.shape
    kvd = N_KV_HEADS * HEAD_DIM
    nblk = tq // WINDOW
    first = (pl.program_id(0) % tiles_per_seq) == 0

    @pl.when(first)
    def _():
        kbuf[0:WINDOW, :] = jnp.zeros((WINDOW, kvd), BF16)
        vbuf[:, 0:WINDOW] = jnp.zeros((kvd, WINDOW), BF16)

    @pl.when(jnp.logical_not(first))
    def _():
        kbuf[0:WINDOW, :] = kbuf[tq:tq + WINDOW, :]
        vbuf[:, 0:WINDOW] = vbuf[:, tq:tq + WINDOW]

    x = x_ref[...]
    h = _rms(x, ng_ref[...]).astype(BF16)
    qkv = lax.dot_general(wqkv_ref[...], h, (((1,), (1,)), ((), ())),
                          preferred_element_type=F32)
    bcol = bqkv_ref[...]

    def head_rows(r0):
        rows = slice(r0, r0 + HEAD_DIM)
        return qkv[rows, :] + jnp.concatenate([bcol[rows, :]] * nblk, axis=1)

    for hd in range(N_HEADS):
        qh = head_rows(hd * HEAD_DIM)
        ms = jnp.mean(qh * qh, axis=0, keepdims=True)
        qbuf[hd * HEAD_DIM:(hd + 1) * HEAD_DIM, :] = (qh * lax.rsqrt(ms + NORM_EPS)).astype(BF16)
    kn = []
    for g in range(N_KV_HEADS):
        kh = head_rows(d + g * HEAD_DIM)
        ms = jnp.mean(kh * kh, axis=0, keepdims=True)
        kn.append(kh * lax.rsqrt(ms + NORM_EPS))
        vbuf[g * HEAD_DIM:(g + 1) * HEAD_DIM, WINDOW:WINDOW + tq] = (
            head_rows(d + kvd + g * HEAD_DIM).astype(BF16))
    kt = jnp.concatenate(kn, axis=0)
    kbuf[WINDOW:WINDOW + tq, :] = (kt.T * kgain_ref[...]).astype(BF16)

    kk = lax.broadcasted_iota(jnp.int32, (2 * WINDOW, WINDOW), 0)
    first_key = jnp.where(first, WINDOW, 0)
    ones_rows = (lax.broadcasted_iota(jnp.int32, (BF16_ROWS, 2 * WINDOW), 0) == 0).astype(BF16)
    pairs = [(j, g) for j in range(nblk) for g in range(N_KV_HEADS)]

    def score_steps(n):
        j, g = pairs[n]
        kw = kbuf[j * WINDOW:(j + 2) * WINDOW, g * HEAD_DIM:(g + 1) * HEAD_DIM]

        def step(c0):
            def run():
                qt = jnp.concatenate(
                    [qbuf[(g * GROUP + c0 // WINDOW + i) * HEAD_DIM:
                          (g * GROUP + c0 // WINDOW + i + 1) * HEAD_DIM, j * WINDOW:(j + 1) * WINDOW]
                     for i in range(MXU_COLS // WINDOW)], axis=1)
                sbufs[n % 2][:, c0:c0 + MXU_COLS] = jnp.dot(kw, qt, preferred_element_type=F32)
            return run
        return [step(c0) for c0 in range(0, GROUP * WINDOW, MXU_COLS)]

    def softmax_steps(n):
        j, g = pairs[n]
        cols = slice(j * WINDOW, (j + 1) * WINDOW)
        band = slice(j * WINDOW, (j + 2) * WINDOW)
        va = jnp.concatenate([vbuf[g * HEAD_DIM:(g + 1) * HEAD_DIM, band], ones_rows], axis=0)

        def step(c0):
            def run():
                ps, ms = [], []
                for i in range(MXU_COLS // WINDOW):
                    blk = slice(c0 + i * WINDOW, c0 + (i + 1) * WINDOW)
                    s = sbufs[n % 2][:, blk] + bias_ref[g, :, blk]
                    if j == 0:
                        s = jnp.where(kk >= first_key, s, NEG_INF)
                    sink = sinks_ref[g * GROUP + c0 // WINDOW + i] * LOG2E
                    m = jnp.maximum(jnp.max(s, axis=0, keepdims=True), sink)
                    ps.append(jnp.exp2(s - m).astype(BF16))
                    ms.append((m, sink))
                ot = jnp.dot(va, jnp.concatenate(ps, axis=1), preferred_element_type=F32)
                for i, (m, sink) in enumerate(ms):
                    hd = g * GROUP + c0 // WINDOW + i
                    denom = ot[HEAD_DIM:HEAD_DIM + 1, i * WINDOW:(i + 1) * WINDOW] + jnp.exp2(sink - m)
                    abuf[hd * HEAD_DIM:(hd + 1) * HEAD_DIM, cols] = (
                        ot[0:HEAD_DIM, i * WINDOW:(i + 1) * WINDOW] * (1.0 / denom)).astype(BF16)
            return run
        return [step(c0) for c0 in range(0, GROUP * WINDOW, MXU_COLS)]

    _interleave(score_steps(0), [])
    for n in range(len(pairs)):
        _interleave(softmax_steps(n), score_steps(n + 1) if n + 1 < len(pairs) else [])

    yt = jnp.dot(wo_ref[...], abuf[...], preferred_element_type=F32)
    o_ref[...] = x + yt.T + bo_ref[...]


def _attn_block(x, seq, ng, w_qkv, b_qkv, q_g, k_g, sinks, w_o, b_o, bias):
    t, d = x.shape
    qkv_dim = w_qkv.shape[1]
    kvd = N_KV_HEADS * HEAD_DIM
    row = lambda v: v.reshape(1, -1)
    kgain = jnp.tile(q_g * k_g * (LOG2E / math.sqrt(HEAD_DIM)), N_KV_HEADS)
    body = functools.partial(_attn_body, tiles_per_seq=seq // TOKEN_TILE)
    return pl.pallas_call(
        body,
        grid=(t // TOKEN_TILE,),
        in_specs=[
            pl.BlockSpec((TOKEN_TILE, d), lambda i: (i, 0)),
            _const_spec((1, d)),
            _const_spec((qkv_dim, d)),
            _const_spec((qkv_dim, LANES)),
            _const_spec((1, kvd)),
            _const_spec((N_KV_HEADS, 2 * WINDOW, GROUP * WINDOW)),
            pl.BlockSpec(memory_space=pltpu.SMEM),
            _const_spec((d, d)),
            _const_spec((1, d)),
        ],
        out_specs=pl.BlockSpec((TOKEN_TILE, d), lambda i: (i, 0)),
        out_shape=jax.ShapeDtypeStruct((t, d), F32),
        scratch_shapes=[
            pltpu.VMEM((d, TOKEN_TILE), BF16),
            pltpu.VMEM((TOKEN_TILE + WINDOW, kvd), BF16),
            pltpu.VMEM((kvd, TOKEN_TILE + WINDOW), BF16),
            pltpu.VMEM((d, TOKEN_TILE), BF16),
            pltpu.VMEM((2 * WINDOW, GROUP * WINDOW), F32),
            pltpu.VMEM((2 * WINDOW, GROUP * WINDOW), F32),
        ],
        compiler_params=_compiler_params(),
        name="attn_block",
    )(x, row(ng), w_qkv.T.astype(BF16), jnp.broadcast_to(b_qkv[:, None], (qkv_dim, LANES)),
      row(kgain), bias, sinks, w_o.T.astype(BF16), row(b_o))


def kernel(x, conv_norm_g, conv_w_in, conv_b_in, conv_dw, conv_dw_b, conv_ln_g, conv_ln_b, conv_w_out, conv_b_out, attn_norm_g, w_qkv, b_qkv, q_norm_g, k_norm_g, sinks, w_o, b_o, rel_bias, mlp_norm_g, w_up, w_down):
    b, s, d = x.shape
    assert s % TOKEN_TILE == 0 and TOKEN_TILE % WINDOW == 0 and d == N_HEADS * HEAD_DIM
    depth = mlp_norm_g.shape[0]
    bias = _band_bias(rel_bias)
    xf = x.reshape(b * s, d)
    for i in range(depth):
        j = i // 2
        if i % 2 == 0:
            xf = _conv_block(xf, s, conv_norm_g[j], conv_w_in[j], conv_b_in[j], conv_dw[j],
                             conv_dw_b[j], conv_ln_g[j], conv_ln_b[j], conv_w_out[j],
                             conv_b_out[j])
        else:
            xf = _attn_block(xf, s, attn_norm_g[j], w_qkv[j], b_qkv[j], q_norm_g[j],
                             k_norm_g[j], sinks[j], w_o[j], b_o[j], bias)
        xf = _mlp(xf, mlp_norm_g[i], w_up[i], w_down[i])
    return xf.reshape(b, s, d)
```

```python
import functools
import math

import numpy as np
import jax
import jax.numpy as jnp
from jax import lax
from jax.experimental import pallas as pl
from jax.experimental.pallas import tpu as pltpu

F32 = jnp.float32
BF16 = jnp.bfloat16

CONV_WIDTH = 31
N_HEADS = 16
N_KV_HEADS = 2
HEAD_DIM = 64
GROUP = N_HEADS // N_KV_HEADS
WINDOW = 128
REL_BUCKETS = 32
REL_MAX_DIST = 128
NORM_EPS = 1e-6
NEG_INF = -1e30
LOG2E = math.log2(math.e)

SUBLANES = 8
LANES = 128
BF16_ROWS = 16
MXU_COLS = 256
VMEM_LIMIT_BYTES = 56 * 1024 * 1024

TOKEN_TILE = 1024
FF_CHUNK = 1024
CONV_HALO = 32
CONV_PHASES = 4
CONV_CHUNK = 32
CONV_TILE = 1024
CONV_SPLIT = 4


def _rms(x, g):
    return x * lax.rsqrt(jnp.mean(x * x, axis=-1, keepdims=True) + NORM_EPS) * g


def _sigmoid(x):
    return 1.0 / (1.0 + jnp.exp(-x))


def _interleave(main, side):
    n, m, j = len(main), len(side), 0
    for i, step in enumerate(main):
        step()
        while j < m and (j + 1) * n <= (i + 1) * m:
            side[j]()
            j += 1
    for step in side[j:]:
        step()


def _const_spec(shape):
    n = len(shape)
    return pl.BlockSpec(shape, lambda i: (0,) * n, pipeline_mode=pl.Buffered(1))


def _compiler_params():
    return pltpu.CompilerParams(dimension_semantics=("arbitrary",),
                                vmem_limit_bytes=VMEM_LIMIT_BYTES)


def _mlp_body(x_ref, g_ref, wu_ref, wd_ref, o_ref):
    x = x_ref[...]
    h = _rms(x, g_ref[...]).astype(BF16)
    acc = x
    for c in range(wu_ref.shape[1] // FF_CHUNK):
        cols = slice(c * FF_CHUNK, (c + 1) * FF_CHUNK)
        hid = jnp.dot(h, wu_ref[:, cols], preferred_element_type=F32)
        hid = jnp.square(jnp.maximum(hid, 0.0)).astype(BF16)
        acc = acc + jnp.dot(hid, wd_ref[cols, :], preferred_element_type=F32)
    o_ref[...] = acc


def _mlp(x, g, w_up, w_down):
    t, d = x.shape
    f = w_up.shape[1]
    return pl.pallas_call(
        _mlp_body,
        grid=(t // TOKEN_TILE,),
        in_specs=[
            pl.BlockSpec((TOKEN_TILE, d), lambda i: (i, 0)),
            _const_spec((1, d)),
            _const_spec((d, f)),
            _const_spec((f, d)),
        ],
        out_specs=pl.BlockSpec((TOKEN_TILE, d), lambda i: (i, 0)),
        out_shape=jax.ShapeDtypeStruct((t, d), F32),
        compiler_params=_compiler_params(),
        name="mlp_block",
    )(x, g.reshape(1, d), w_up.astype(BF16), w_down.astype(BF16))


def _conv_body(x_ref, ng_ref, win_ref, bin_ref, dw_ref, dwb_ref, lng_ref, lnb_ref,
               wout_ref, bout_ref, o_ref, *bufs, tiles_per_seq):
    tm, d = x_ref.shape
    nslab = d // LANES
    part = tm // CONV_SPLIT
    span = CONV_PHASES * CONV_CHUNK
    first = (pl.program_id(0) % tiles_per_seq) == 0
    ubufs, cbufs = bufs[:CONV_SPLIT], bufs[CONV_SPLIT:]

    @pl.when(first)
    def _():
        ubufs[0][:, 0:CONV_HALO, :] = jnp.zeros((nslab, CONV_HALO, LANES), F32)

    @pl.when(jnp.logical_not(first))
    def _():
        ubufs[0][:, 0:CONV_HALO, :] = ubufs[-1][:, part:part + CONV_HALO, :]

    def glu_steps(gi):
        rows = slice(gi * part, (gi + 1) * part)
        h = _rms(x_ref[rows, :], ng_ref[...]).astype(BF16)

        def step(c0):
            def run():
                a_cols = slice(c0, c0 + MXU_COLS)
                g_cols = slice(d + c0, d + c0 + MXU_COLS)
                a = jnp.dot(h, win_ref[:, a_cols], preferred_element_type=F32) + bin_ref[:, a_cols]
                g = jnp.dot(h, win_ref[:, g_cols], preferred_element_type=F32) + bin_ref[:, g_cols]
                glu = a * _sigmoid(g)
                for i in range(MXU_COLS // LANES):
                    piece = glu[:, i * LANES:(i + 1) * LANES]
                    ubufs[gi][c0 // LANES + i, CONV_HALO:CONV_HALO + part, :] = piece
                    if gi + 1 < CONV_SPLIT:
                        ubufs[gi + 1][c0 // LANES + i, 0:CONV_HALO, :] = piece[part - CONV_HALO:, :]
            return run
        return [step(c0) for c0 in range(0, d, MXU_COLS)]

    def conv_steps(gi):
        def step(s):
            def run():
                lanes = slice(s * LANES, (s + 1) * LANES)
                for t0 in range(0, part, span):
                    acc = [jnp.zeros((CONV_CHUNK, LANES), F32) + dwb_ref[:, lanes]] * CONV_PHASES
                    for c in range(-(CONV_WIDTH - 1), CONV_PHASES):
                        lc = ubufs[gi][s, pl.ds(CONV_HALO + t0 + c, CONV_CHUNK, stride=CONV_PHASES), :]
                        for ph in range(CONV_PHASES):
                            k = c - ph + CONV_WIDTH - 1
                            if 0 <= k < CONV_WIDTH:
                                acc[ph] = acc[ph] + lc * dw_ref[k:k + 1, lanes]
                    for ph in range(CONV_PHASES):
                        cbufs[gi][s, pl.ds(t0 + ph, CONV_CHUNK, stride=CONV_PHASES), :] = acc[ph]
            return run
        return [step(s) for s in range(nslab)]

    def out_steps(gi):
        rows = slice(gi * part, (gi + 1) * part)
        c = jnp.concatenate([cbufs[gi][s] for s in range(nslab)], axis=1)
        mu = jnp.mean(c, axis=-1, keepdims=True)
        cc = c - mu
        y = cc * lax.rsqrt(jnp.mean(cc * cc, axis=-1, keepdims=True) + NORM_EPS)
        y = y * lng_ref[...] + lnb_ref[...]
        y = (y * _sigmoid(y)).astype(BF16)

        def step(c0):
            def run():
                cols = slice(c0, c0 + MXU_COLS)
                o_ref[rows, cols] = (x_ref[rows, cols] + bout_ref[:, cols]
                                     + jnp.dot(y, wout_ref[:, cols], preferred_element_type=F32))
            return run
        return [step(c0) for c0 in range(0, d, MXU_COLS)]

    _interleave(glu_steps(0), [])
    for gi in range(CONV_SPLIT):
        mxu_side = []
        if gi > 0:
            mxu_side += out_steps(gi - 1)
        if gi + 1 < CONV_SPLIT:
            mxu_side += glu_steps(gi + 1)
        _interleave(conv_steps(gi), mxu_side)
    _interleave(out_steps(CONV_SPLIT - 1), [])


def _conv_block(x, seq, ng, w_in, b_in, dw, dw_b, ln_g, ln_b, w_out, b_out):
    t, d = x.shape
    row = lambda v: v.reshape(1, -1)
    body = functools.partial(_conv_body, tiles_per_seq=seq // CONV_TILE)
    part = CONV_TILE // CONV_SPLIT
    return pl.pallas_call(
        body,
        grid=(t // CONV_TILE,),
        in_specs=[
            pl.BlockSpec((CONV_TILE, d), lambda i: (i, 0)),
            _const_spec((1, d)),
            _const_spec((d, 2 * d)),
            _const_spec((1, 2 * d)),
            _const_spec((CONV_WIDTH, d)),
            _const_spec((1, d)),
            _const_spec((1, d)),
            _const_spec((1, d)),
            _const_spec((d, d)),
            _const_spec((1, d)),
        ],
        out_specs=pl.BlockSpec((CONV_TILE, d), lambda i: (i, 0)),
        out_shape=jax.ShapeDtypeStruct((t, d), F32),
        scratch_shapes=(
            [pltpu.VMEM((d // LANES, part + CONV_HALO, LANES), F32)] * CONV_SPLIT
            + [pltpu.VMEM((d // LANES, part, LANES), F32)] * CONV_SPLIT),
        compiler_params=_compiler_params(),
        name="conv_block",
    )(x, row(ng), w_in.astype(BF16), row(b_in), dw, row(dw_b), row(ln_g), row(ln_b),
      w_out.astype(BF16), row(b_out))


def _t5_causal_bucket(dist):
    n = np.maximum(dist, 0)
    max_exact = REL_BUCKETS // 2
    large = max_exact + (np.log(np.maximum(n, 1).astype(np.float32) / max_exact)
                         / math.log(REL_MAX_DIST / max_exact)
                         * (REL_BUCKETS - max_exact)).astype(np.int32)
    large = np.minimum(large, REL_BUCKETS - 1)
    return np.where(n < max_exact, n, large).astype(np.int32)


def _bias_body(rel_ref, bucket_ref, o_ref):
    g = pl.program_id(0)
    bucket = bucket_ref[...]
    for hh in range(GROUP):
        acc = jnp.full(bucket.shape, NEG_INF, F32)
        for b in range(REL_BUCKETS):
            acc = jnp.where(bucket == b, rel_ref[b, g * GROUP + hh] * LOG2E, acc)
        o_ref[0, :, hh * WINDOW:(hh + 1) * WINDOW] = acc


def _band_bias(rel_bias):
    k_loc = np.arange(2 * WINDOW)[:, None]
    q_loc = np.arange(WINDOW)[None, :]
    dist = q_loc + WINDOW - k_loc
    bucket = np.where((dist >= 0) & (dist < WINDOW), _t5_causal_bucket(dist), -1)
    return pl.pallas_call(
        _bias_body,
        grid=(N_KV_HEADS,),
        in_specs=[
            pl.BlockSpec(memory_space=pltpu.SMEM),
            pl.BlockSpec((2 * WINDOW, WINDOW), lambda i: (0, 0)),
        ],
        out_specs=pl.BlockSpec((1, 2 * WINDOW, GROUP * WINDOW), lambda i: (i, 0, 0)),
        out_shape=jax.ShapeDtypeStruct((N_KV_HEADS, 2 * WINDOW, GROUP * WINDOW), F32),
        name="band_bias",
    )(rel_bias, jnp.asarray(bucket.astype(np.int32)))


def _attn_body(x_ref, ng_ref, wqkv_ref, bqkv_ref, kgain_ref, bias_ref, sinks_ref, wo_ref,
               bo_ref, o_ref, qbuf, kbuf, vbuf, abuf, *sbufs, tiles_per_seq):
    tq, d = x_ref.shape
    kvd = N_KV_HEADS * HEAD_DIM
    nblk = tq // WINDOW
    first = (pl.program_id(0) % tiles_per_seq) == 0

    @pl.when(first)
    def _():
        kbuf[0:WINDOW, :] = jnp.zeros((WINDOW, kvd), BF16)
        vbuf[:, 0:WINDOW] = jnp.zeros((kvd, WINDOW), BF16)

    @pl.when(jnp.logical_not(first))
    def _():
        kbuf[0:WINDOW, :] = kbuf[tq:tq + WINDOW, :]
        vbuf[:, 0:WINDOW] = vbuf[:, tq:tq + WINDOW]

    x = x_ref[...]
    h = _rms(x, ng_ref[...]).astype(BF16)
    qkv = lax.dot_general(wqkv_ref[...], h, (((1,), (1,)), ((), ())),
                          preferred_element_type=F32)
    bcol = bqkv_ref[...]

    def head_rows(r0):
        rows = slice(r0, r0 + HEAD_DIM)
        return qkv[rows, :] + jnp.concatenate([bcol[rows, :]] * nblk, axis=1)

    for hd in range(N_HEADS):
        qh = head_rows(hd * HEAD_DIM)
        ms = jnp.mean(qh * qh, axis=0, keepdims=True)
        qbuf[hd * HEAD_DIM:(hd + 1) * HEAD_DIM, :] = (qh * lax.rsqrt(ms + NORM_EPS)).astype(BF16)
    kn = []
    for g in range(N_KV_HEADS):
        kh = head_rows(d + g * HEAD_DIM)
        ms = jnp.mean(kh * kh, axis=0, keepdims=True)
        kn.append(kh * lax.rsqrt(ms + NORM_EPS))
        vbuf[g * HEAD_DIM:(g + 1) * HEAD_DIM, WINDOW:WINDOW + tq] = (
            head_rows(d + kvd + g * HEAD_DIM).astype(BF16))
    kt = jnp.concatenate(kn, axis=0)
    kbuf[WINDOW:WINDOW + tq, :] = (kt.T * kgain_ref[...]).astype(BF16)

    kk = lax.broadcasted_iota(jnp.int32, (2 * WINDOW, WINDOW), 0)
    first_key = jnp.where(first, WINDOW, 0)
    ones_rows = (lax.broadcasted_iota(jnp.int32, (BF16_ROWS, 2 * WINDOW), 0) == 0).astype(BF16)
    pairs = [(j, g) for j in range(nblk) for g in range(N_KV_HEADS)]

    def score_steps(n):
        j, g = pairs[n]
        kw = kbuf[j * WINDOW:(j + 2) * WINDOW, g * HEAD_DIM:(g + 1) * HEAD_DIM]

        def step(c0):
            def run():
                qt = jnp.concatenate(
                    [qbuf[(g * GROUP + c0 // WINDOW + i) * HEAD_DIM:
                          (g * GROUP + c0 // WINDOW + i + 1) * HEAD_DIM, j * WINDOW:(j + 1) * WINDOW]
                     for i in range(MXU_COLS // WINDOW)], axis=1)
                sbufs[n % 2][:, c0:c0 + MXU_COLS] = jnp.dot(kw, qt, preferred_element_type=F32)
            return run
        return [step(c0) for c0 in range(0, GROUP * WINDOW, MXU_COLS)]

    def softmax_steps(n):
        j, g = pairs[n]
        cols = slice(j * WINDOW, (j + 1) * WINDOW)
        band = slice(j * WINDOW, (j + 2) * WINDOW)
        va = jnp.concatenate([vbuf[g * HEAD_DIM:(g + 1) * HEAD_DIM, band], ones_rows], axis=0)

        def step(c0):
            def run():
                ps, ms = [], []
                for i in range(MXU_COLS // WINDOW):
                    blk = slice(c0 + i * WINDOW, c0 + (i + 1) * WINDOW)
                    s = sbufs[n % 2][:, blk] + bias_ref[g, :, blk]
                    if j == 0:
                        s = jnp.where(kk >= first_key, s, NEG_INF)
                    sink = sinks_ref[g * GROUP + c0 // WINDOW + i] * LOG2E
                    m = jnp.maximum(jnp.max(s, axis=0, keepdims=True), sink)
                    ps.append(jnp.exp2(s - m).astype(BF16))
                    ms.append((m, sink))
                ot = jnp.dot(va, jnp.concatenate(ps, axis=1), preferred_element_type=F32)
                for i, (m, sink) in enumerate(ms):
                    hd = g * GROUP + c0 // WINDOW + i
                    denom = ot[HEAD_DIM:HEAD_DIM + 1, i * WINDOW:(i + 1) * WINDOW] + jnp.exp2(sink - m)
                    abuf[hd * HEAD_DIM:(hd + 1) * HEAD_DIM, cols] = (
                        ot[0:HEAD_DIM, i * WINDOW:(i + 1) * WINDOW] * (1.0 / denom)).astype(BF16)
            return run
        return [step(c0) for c0 in range(0, GROUP * WINDOW, MXU_COLS)]

    _interleave(score_steps(0), [])
    for n in range(len(pairs)):
        _interleave(softmax_steps(n), score_steps(n + 1) if n + 1 < len(pairs) else [])

    yt = jnp.dot(wo_ref[...], abuf[...], preferred_element_type=F32)
    o_ref[...] = x + yt.T + bo_ref[...]


def _attn_block(x, seq, ng, w_qkv, b_qkv, q_g, k_g, sinks, w_o, b_o, bias):
    t, d = x.shape
    qkv_dim = w_qkv.shape[1]
    kvd = N_KV_HEADS * HEAD_DIM
    row = lambda v: v.reshape(1, -1)
    kgain = jnp.tile(q_g * k_g * (LOG2E / math.sqrt(HEAD_DIM)), N_KV_HEADS)
    body = functools.partial(_attn_body, tiles_per_seq=seq // TOKEN_TILE)
    return pl.pallas_call(
        body,
        grid=(t // TOKEN_TILE,),
        in_specs=[
            pl.BlockSpec((TOKEN_TILE, d), lambda i: (i, 0)),
            _const_spec((1, d)),
            _const_spec((qkv_dim, d)),
            _const_spec((qkv_dim, LANES)),
            _const_spec((1, kvd)),
            _const_spec((N_KV_HEADS, 2 * WINDOW, GROUP * WINDOW)),
            pl.BlockSpec(memory_space=pltpu.SMEM),
            _const_spec((d, d)),
            _const_spec((1, d)),
        ],
        out_specs=pl.BlockSpec((TOKEN_TILE, d), lambda i: (i, 0)),
        out_shape=jax.ShapeDtypeStruct((t, d), F32),
        scratch_shapes=[
            pltpu.VMEM((d, TOKEN_TILE), BF16),
            pltpu.VMEM((TOKEN_TILE + WINDOW, kvd), BF16),
            pltpu.VMEM((kvd, TOKEN_TILE + WINDOW), BF16),
            pltpu.VMEM((d, TOKEN_TILE), BF16),
            pltpu.VMEM((2 * WINDOW, GROUP * WINDOW), F32),
            pltpu.VMEM((2 * WINDOW, GROUP * WINDOW), F32),
        ],
        compiler_params=_compiler_params(),
        name="attn_block",
    )(x, row(ng), w_qkv.T.astype(BF16), jnp.broadcast_to(b_qkv[:, None], (qkv_dim, LANES)),
      row(kgain), bias, sinks, w_o.T.astype(BF16), row(b_o))


def kernel(x, conv_norm_g, conv_w_in, conv_b_in, conv_dw, conv_dw_b, conv_ln_g, conv_ln_b, conv_w_out, conv_b_out, attn_norm_g, w_qkv, b_qkv, q_norm_g, k_norm_g, sinks, w_o, b_o, rel_bias, mlp_norm_g, w_up, w_down):
    b, s, d = x.shape
    assert s % TOKEN_TILE == 0 and TOKEN_TILE % WINDOW == 0 and d == N_HEADS * HEAD_DIM
    depth = mlp_norm_g.shape[0]
    bias = _band_bias(rel_bias)
    xf = x.reshape(b * s, d)
    for i in range(depth):
        j = i // 2
        if i % 2 == 0:
            xf = _conv_block(xf, s, conv_norm_g[j], conv_w_in[j], conv_b_in[j], conv_dw[j],
                             conv_dw_b[j], conv_ln_g[j], conv_ln_b[j], conv_w_out[j],
                             conv_b_out[j])
        else:
            xf = _attn_block(xf, s, attn_norm_g[j], w_qkv[j], b_qkv[j], q_norm_g[j],
                             k_norm_g[j], sinks[j], w_o[j], b_o[j], bias)
        xf = _mlp(xf, mlp_norm_g[i], w_up[i], w_down[i])
    return xf.reshape(b, s, d)
```

```python
import functools
import math

import numpy as np
import jax
import jax.numpy as jnp
from jax import lax
from jax.experimental import pallas as pl
from jax.experimental.pallas import tpu as pltpu

F32 = jnp.float32
BF16 = jnp.bfloat16

CONV_WIDTH = 31
N_HEADS = 16
N_KV_HEADS = 2
HEAD_DIM = 64
GROUP = N_HEADS // N_KV_HEADS
WINDOW = 128
REL_BUCKETS = 32
REL_MAX_DIST = 128
NORM_EPS = 1e-6
NEG_INF = -1e30
LOG2E = math.log2(math.e)

SUBLANES = 8
LANES = 128
BF16_ROWS = 16
MXU_COLS = 256
VMEM_LIMIT_BYTES = 58 * 1024 * 1024

TOKEN_TILE = 1024
FF_CHUNK = 512
CONV_HALO = 32
CONV_PHASES = 4
CONV_CHUNK = 32
CONV_TILE = 1024
CONV_SPLIT = 4


def _rms(x, g):
    return x * lax.rsqrt(jnp.mean(x * x, axis=-1, keepdims=True) + NORM_EPS) * g


def _sigmoid(x):
    return 1.0 / (1.0 + jnp.exp(-x))


def _interleave(main, side):
    n, m, j = len(main), len(side), 0
    for i, step in enumerate(main):
        step()
        while j < m and (j + 1) * n <= (i + 1) * m:
            side[j]()
            j += 1
    for step in side[j:]:
        step()


def _const_spec(shape):
    n = len(shape)
    return pl.BlockSpec(shape, lambda i: (0,) * n, pipeline_mode=pl.Buffered(1))


def _layer_spec(shape, layer):
    n = len(shape)
    return pl.BlockSpec((None,) + tuple(shape), lambda i: (layer,) + (0,) * n,
                        pipeline_mode=pl.Buffered(1))


def _compiler_params():
    return pltpu.CompilerParams(dimension_semantics=("arbitrary",),
                                vmem_limit_bytes=VMEM_LIMIT_BYTES)


def _mlp_body(x_ref, g_ref, wu_ref, wd_ref, o_ref):
    x = x_ref[...]
    h = _rms(x, g_ref[...]).astype(BF16)
    acc = x
    for c in range(wu_ref.shape[1] // FF_CHUNK):
        cols = slice(c * FF_CHUNK, (c + 1) * FF_CHUNK)
        hid = jnp.dot(h, wu_ref[:, cols].astype(BF16), preferred_element_type=F32)
        hid = jnp.square(jnp.maximum(hid, 0.0)).astype(BF16)
        acc = acc + jnp.dot(hid, wd_ref[cols, :].astype(BF16), preferred_element_type=F32)
    o_ref[...] = acc


def _mlp(x, g, w_up, w_down, layer):
    t, d = x.shape
    f = w_up.shape[2]
    return pl.pallas_call(
        _mlp_body,
        grid=(t // TOKEN_TILE,),
        in_specs=[
            pl.BlockSpec((TOKEN_TILE, d), lambda i: (i, 0)),
            _layer_spec((1, d), layer),
            _layer_spec((d, f), layer),
            _layer_spec((f, d), layer),
        ],
        out_specs=pl.BlockSpec((TOKEN_TILE, d), lambda i: (i, 0)),
        out_shape=jax.ShapeDtypeStruct((t, d), F32),
        compiler_params=_compiler_params(),
        name="mlp_block",
    )(x, g.reshape(g.shape[0], 1, d), w_up, w_down)


def _conv_body(x_ref, ng_ref, win_f32, bin_ref, dw_ref, dwb_ref, lng_ref, lnb_ref,
               wout_f32, bout_ref, o_ref, win_ref, wout_ref, *bufs, tiles_per_seq):
    tm, d = x_ref.shape
    nslab = d // LANES
    part = tm // CONV_SPLIT
    span = CONV_PHASES * CONV_CHUNK
    first = (pl.program_id(0) % tiles_per_seq) == 0
    ubufs, cbufs = bufs[:CONV_SPLIT], bufs[CONV_SPLIT:]

    @pl.when(pl.program_id(0) == 0)
    def _():
        win_ref[...] = win_f32[...].astype(BF16)
        wout_ref[...] = wout_f32[...].astype(BF16)

    @pl.when(first)
    def _():
        ubufs[0][:, 0:CONV_HALO, :] = jnp.zeros((nslab, CONV_HALO, LANES), F32)

    @pl.when(jnp.logical_not(first))
    def _():
        ubufs[0][:, 0:CONV_HALO, :] = ubufs[-1][:, part:part + CONV_HALO, :]

    def glu_steps(gi):
        rows = slice(gi * part, (gi + 1) * part)
        h = _rms(x_ref[rows, :], ng_ref[...]).astype(BF16)

        def step(c0):
            def run():
                a_cols = slice(c0, c0 + MXU_COLS)
                g_cols = slice(d + c0, d + c0 + MXU_COLS)
                a = jnp.dot(h, win_ref[:, a_cols], preferred_element_type=F32) + bin_ref[:, a_cols]
                g = jnp.dot(h, win_ref[:, g_cols], preferred_element_type=F32) + bin_ref[:, g_cols]
                glu = a * _sigmoid(g)
                for i in range(MXU_COLS // LANES):
                    piece = glu[:, i * LANES:(i + 1) * LANES]
                    ubufs[gi][c0 // LANES + i, CONV_HALO:CONV_HALO + part, :] = piece
                    if gi + 1 < CONV_SPLIT:
                        ubufs[gi + 1][c0 // LANES + i, 0:CONV_HALO, :] = piece[part - CONV_HALO:, :]
            return run
        return [step(c0) for c0 in range(0, d, MXU_COLS)]

    def conv_steps(gi):
        def step(s):
            def run():
                lanes = slice(s * LANES, (s + 1) * LANES)
                for t0 in range(0, part, span):
                    acc = [jnp.zeros((CONV_CHUNK, LANES), F32) + dwb_ref[:, lanes]] * CONV_PHASES
                    for c in range(-(CONV_WIDTH - 1), CONV_PHASES):
                        lc = ubufs[gi][s, pl.ds(CONV_HALO + t0 + c, CONV_CHUNK, stride=CONV_PHASES), :]
                        for ph in range(CONV_PHASES):
                            k = c - ph + CONV_WIDTH - 1
                            if 0 <= k < CONV_WIDTH:
                                acc[ph] = acc[ph] + lc * dw_ref[k:k + 1, lanes]
                    for ph in range(CONV_PHASES):
                        cbufs[gi][s, pl.ds(t0 + ph, CONV_CHUNK, stride=CONV_PHASES), :] = acc[ph]
            return run
        return [step(s) for s in range(nslab)]

    def out_steps(gi):
        rows = slice(gi * part, (gi + 1) * part)
        c = jnp.concatenate([cbufs[gi][s] for s in range(nslab)], axis=1)
        mu = jnp.mean(c, axis=-1, keepdims=True)
        cc = c - mu
        y = cc * lax.rsqrt(jnp.mean(cc * cc, axis=-1, keepdims=True) + NORM_EPS)
        y = y * lng_ref[...] + lnb_ref[...]
        y = (y * _sigmoid(y)).astype(BF16)

        def step(c0):
            def run():
                cols = slice(c0, c0 + MXU_COLS)
                o_ref[rows, cols] = (x_ref[rows, cols] + bout_ref[:, cols]
                                     + jnp.dot(y, wout_ref[:, cols], preferred_element_type=F32))
            return run
        return [step(c0) for c0 in range(0, d, MXU_COLS)]

    _interleave(glu_steps(0), [])
    for gi in range(CONV_SPLIT):
        mxu_side = []
        if gi > 0:
            mxu_side += out_steps(gi - 1)
        if gi + 1 < CONV_SPLIT:
            mxu_side += glu_steps(gi + 1)
        _interleave(conv_steps(gi), mxu_side)
    _interleave(out_steps(CONV_SPLIT - 1), [])


def _conv_block(x, seq, layer, ng, w_in, b_in, dw, dw_b, ln_g, ln_b, w_out, b_out):
    t, d = x.shape
    row = lambda v: v.reshape(1, -1)
    body = functools.partial(_conv_body, tiles_per_seq=seq // CONV_TILE)
    part = CONV_TILE // CONV_SPLIT
    return pl.pallas_call(
        body,
        grid=(t // CONV_TILE,),
        in_specs=[
            pl.BlockSpec((CONV_TILE, d), lambda i: (i, 0)),
            _const_spec((1, d)),
            _layer_spec((d, 2 * d), layer),
            _const_spec((1, 2 * d)),
            _const_spec((CONV_WIDTH, d)),
            _const_spec((1, d)),
            _const_spec((1, d)),
            _const_spec((1, d)),
            _layer_spec((d, d), layer),
            _const_spec((1, d)),
        ],
        out_specs=pl.BlockSpec((CONV_TILE, d), lambda i: (i, 0)),
        out_shape=jax.ShapeDtypeStruct((t, d), F32),
        scratch_shapes=(
            [pltpu.VMEM((d, 2 * d), BF16), pltpu.VMEM((d, d), BF16)]
            + [pltpu.VMEM((d // LANES, part + CONV_HALO, LANES), F32)] * CONV_SPLIT
            + [pltpu.VMEM((d // LANES, part, LANES), F32)] * CONV_SPLIT),
        compiler_params=_compiler_params(),
        name="conv_block",
    )(x, row(ng), w_in, row(b_in), dw, row(dw_b), row(ln_g), row(ln_b), w_out, row(b_out))


def _t5_causal_bucket(dist):
    n = np.maximum(dist, 0)
    max_exact = REL_BUCKETS // 2
    large = max_exact + (np.log(np.maximum(n, 1).astype(np.float32) / max_exact)
                         / math.log(REL_MAX_DIST / max_exact)
                         * (REL_BUCKETS - max_exact)).astype(np.int32)
    large = np.minimum(large, REL_BUCKETS - 1)
    return np.where(n < max_exact, n, large).astype(np.int32)


def _bias_body(rel_ref, bucket_ref, o_ref):
    g = pl.program_id(0)
    bucket = bucket_ref[...]
    for hh in range(GROUP):
        acc = jnp.full(bucket.shape, NEG_INF, F32)
        for b in range(REL_BUCKETS):
            acc = jnp.where(bucket == b, rel_ref[b, g * GROUP + hh] * LOG2E, acc)
        o_ref[0, :, hh * WINDOW:(hh + 1) * WINDOW] = acc


def _band_bias(rel_bias):
    k_loc = np.arange(2 * WINDOW)[:, None]
    q_loc = np.arange(WINDOW)[None, :]
    dist = q_loc + WINDOW - k_loc
    bucket = np.where((dist >= 0) & (dist < WINDOW), _t5_causal_bucket(dist), -1)
    return pl.pallas_call(
        _bias_body,
        grid=(N_KV_HEADS,),
        in_specs=[
            pl.BlockSpec(memory_space=pltpu.SMEM),
            pl.BlockSpec((2 * WINDOW, WINDOW), lambda i: (0, 0)),
        ],
        out_specs=pl.BlockSpec((1, 2 * WINDOW, GROUP * WINDOW), lambda i: (i, 0, 0)),
        out_shape=jax.ShapeDtypeStruct((N_KV_HEADS, 2 * WINDOW, GROUP * WINDOW), F32),
        name="band_bias",
    )(rel_bias, jnp.asarray(bucket.astype(np.int32)))


def _attn_body(x_ref, ng_ref, wqkv_f32, bqkv_ref, kgain_ref, bias_ref, sinks_ref, wo_f32,
               bo_ref, o_ref, wqkv_ref, wo_ref, qbuf, kbuf, vbuf, abuf, *sbufs, tiles_per_seq):
    tq, d = x_ref.shape
    kvd = N_KV_HEADS * HEAD_DIM
    nblk = tq // WINDOW
    first = (pl.program_id(0) % tiles_per_seq) == 0

    @pl.when(pl.program_id(0) == 0)
    def _():
        wqkv_ref[...] = wqkv_f32[...].T.astype(BF16)
        wo_ref[...] = wo_f32[...].T.astype(BF16)

    @pl.when(first)
    def _():
        kbuf[0:WINDOW, :] = jnp.zeros((WINDOW, kvd), BF16)
        vbuf[:, 0:WINDOW] = jnp.zeros((kvd, WINDOW), BF16)

    @pl.when(jnp.logical_not(first))
    def _():
        kbuf[0:WINDOW, :] = kbuf[tq:tq + WINDOW, :]
        vbuf[:, 0:WINDOW] = vbuf[:, tq:tq + WINDOW]

    x = x_ref[...]
    h = _rms(x, ng_ref[...]).astype(BF16)
    qkv = lax.dot_general(wqkv_ref[...], h, (((1,), (1,)), ((), ())),
                          preferred_element_type=F32)
    bcol = bqkv_ref[...]

    def head_rows(r0):
        rows = slice(r0, r0 + HEAD_DIM)
        return qkv[rows, :] + jnp.concatenate([bcol[rows, :]] * nblk, axis=1)

    for hd in range(N_HEADS):
        qh = head_rows(hd * HEAD_DIM)
        ms = jnp.mean(qh * qh, axis=0, keepdims=True)
        qbuf[hd * HEAD_DIM:(hd + 1) * HEAD_DIM, :] = (qh * lax.rsqrt(ms + NORM_EPS)).astype(BF16)
    kn = []
    for g in range(N_KV_HEADS):
        kh = head_rows(d + g * HEAD_DIM)
        ms = jnp.mean(kh * kh, axis=0, keepdims=True)
        kn.append(kh * lax.rsqrt(ms + NORM_EPS))
        vbuf[g * HEAD_DIM:(g + 1) * HEAD_DIM, WINDOW:WINDOW + tq] = (
            head_rows(d + kvd + g * HEAD_DIM).astype(BF16))
    kt = jnp.concatenate(kn, axis=0)
    kbuf[WINDOW:WINDOW + tq, :] = (kt.T * kgain_ref[...]).astype(BF16)

    kk = lax.broadcasted_iota(jnp.int32, (2 * WINDOW, WINDOW), 0)
    first_key = jnp.where(first, WINDOW, 0)
    ones_rows = (lax.broadcasted_iota(jnp.int32, (BF16_ROWS, 2 * WINDOW), 0) == 0).astype(BF16)
    pairs = [(j, g) for j in range(nblk) for g in range(N_KV_HEADS)]

    def score_steps(n):
        j, g = pairs[n]
        kw = kbuf[j * WINDOW:(j + 2) * WINDOW, g * HEAD_DIM:(g + 1) * HEAD_DIM]

        def step(c0):
            def run():
                qt = jnp.concatenate(
                    [qbuf[(g * GROUP + c0 // WINDOW + i) * HEAD_DIM:
                          (g * GROUP + c0 // WINDOW + i + 1) * HEAD_DIM, j * WINDOW:(j + 1) * WINDOW]
                     for i in range(MXU_COLS // WINDOW)], axis=1)
                sbufs[n % 2][:, c0:c0 + MXU_COLS] = jnp.dot(kw, qt, preferred_element_type=F32)
            return run
        return [step(c0) for c0 in range(0, GROUP * WINDOW, MXU_COLS)]

    def softmax_steps(n):
        j, g = pairs[n]
        cols = slice(j * WINDOW, (j + 1) * WINDOW)
        band = slice(j * WINDOW, (j + 2) * WINDOW)
        va = jnp.concatenate([vbuf[g * HEAD_DIM:(g + 1) * HEAD_DIM, band], ones_rows], axis=0)

        def step(c0):
            def run():
                ps, ms = [], []
                for i in range(MXU_COLS // WINDOW):
                    blk = slice(c0 + i * WINDOW, c0 + (i + 1) * WINDOW)
                    s = sbufs[n % 2][:, blk] + bias_ref[g, :, blk]
                    if j == 0:
                        s = jnp.where(kk >= first_key, s, NEG_INF)
                    sink = sinks_ref[g * GROUP + c0 // WINDOW + i] * LOG2E
                    m = jnp.maximum(jnp.max(s, axis=0, keepdims=True), sink)
                    ps.append(jnp.exp2(s - m).astype(BF16))
                    ms.append((m, sink))
                ot = jnp.dot(va, jnp.concatenate(ps, axis=1), preferred_element_type=F32)
                for i, (m, sink) in enumerate(ms):
                    hd = g * GROUP + c0 // WINDOW + i
                    denom = ot[HEAD_DIM:HEAD_DIM + 1, i * WINDOW:(i + 1) * WINDOW] + jnp.exp2(sink - m)
                    abuf[hd * HEAD_DIM:(hd + 1) * HEAD_DIM, cols] = (
                        ot[0:HEAD_DIM, i * WINDOW:(i + 1) * WINDOW] * (1.0 / denom)).astype(BF16)
            return run
        return [step(c0) for c0 in range(0, GROUP * WINDOW, MXU_COLS)]

    _interleave(score_steps(0), [])
    for n in range(len(pairs)):
        _interleave(softmax_steps(n), score_steps(n + 1) if n + 1 < len(pairs) else [])

    yt = jnp.dot(wo_ref[...], abuf[...], preferred_element_type=F32)
    o_ref[...] = x + yt.T + bo_ref[...]


def _attn_block(x, seq, layer, ng, w_qkv, b_qkv, q_g, k_g, sinks, w_o, b_o, bias):
    t, d = x.shape
    qkv_dim = w_qkv.shape[2]
    kvd = N_KV_HEADS * HEAD_DIM
    row = lambda v: v.reshape(1, -1)
    kgain = jnp.tile(q_g * k_g * (LOG2E / math.sqrt(HEAD_DIM)), N_KV_HEADS)
    body = functools.partial(_attn_body, tiles_per_seq=seq // TOKEN_TILE)
    return pl.pallas_call(
        body,
        grid=(t // TOKEN_TILE,),
        in_specs=[
            pl.BlockSpec((TOKEN_TILE, d), lambda i: (i, 0)),
            _const_spec((1, d)),
            _layer_spec((d, qkv_dim), layer),
            _const_spec((qkv_dim, LANES)),
            _const_spec((1, kvd)),
            _const_spec((N_KV_HEADS, 2 * WINDOW, GROUP * WINDOW)),
            pl.BlockSpec(memory_space=pltpu.SMEM),
            _layer_spec((d, d), layer),
            _const_spec((1, d)),
        ],
        out_specs=pl.BlockSpec((TOKEN_TILE, d), lambda i: (i, 0)),
        out_shape=jax.ShapeDtypeStruct((t, d), F32),
        scratch_shapes=[
            pltpu.VMEM((qkv_dim, d), BF16),
            pltpu.VMEM((d, d), BF16),
            pltpu.VMEM((d, TOKEN_TILE), BF16),
            pltpu.VMEM((TOKEN_TILE + WINDOW, kvd), BF16),
            pltpu.VMEM((kvd, TOKEN_TILE + WINDOW), BF16),
            pltpu.VMEM((d, TOKEN_TILE), BF16),
            pltpu.VMEM((2 * WINDOW, GROUP * WINDOW), F32),
            pltpu.VMEM((2 * WINDOW, GROUP * WINDOW), F32),
        ],
        compiler_params=_compiler_params(),
        name="attn_block",
    )(x, row(ng), w_qkv, jnp.broadcast_to(b_qkv[:, None], (qkv_dim, LANES)),
      row(kgain), bias, sinks, w_o, row(b_o))


def kernel(x, conv_norm_g, conv_w_in, conv_b_in, conv_dw, conv_dw_b, conv_ln_g, conv_ln_b, conv_w_out, conv_b_out, attn_norm_g, w_qkv, b_qkv, q_norm_g, k_norm_g, sinks, w_o, b_o, rel_bias, mlp_norm_g, w_up, w_down):
    b, s, d = x.shape
    assert s % TOKEN_TILE == 0 and TOKEN_TILE % WINDOW == 0 and d == N_HEADS * HEAD_DIM
    depth = mlp_norm_g.shape[0]
    bias = _band_bias(rel_bias)
    xf = x.reshape(b * s, d)
    for i in range(depth):
        j = i // 2
        if i % 2 == 0:
            xf = _conv_block(xf, s, j, conv_norm_g[j], conv_w_in, conv_b_in[j], conv_dw[j],
                             conv_dw_b[j], conv_ln_g[j], conv_ln_b[j], conv_w_out,
                             conv_b_out[j])
        else:
            xf = _attn_block(xf, s, j, attn_norm_g[j], w_qkv, b_qkv[j], q_norm_g[j],
                             k_norm_g[j], sinks[j], w_o, b_o[j], bias)
        xf = _mlp(xf, mlp_norm_g, w_up, w_down, i)
    return xf.reshape(b, s, d)
```

```python
import functools
import math

import numpy as np
import jax
import jax.numpy as jnp
from jax import lax
from jax.experimental import pallas as pl
from jax.experimental.pallas import tpu as pltpu

F32 = jnp.float32
BF16 = jnp.bfloat16

CONV_WIDTH = 31
N_HEADS = 16
N_KV_HEADS = 2
HEAD_DIM = 64
GROUP = N_HEADS // N_KV_HEADS
WINDOW = 128
REL_BUCKETS = 32
REL_MAX_DIST = 128
NORM_EPS = 1e-6
NEG_INF = -1e30
LOG2E = math.log2(math.e)

SUBLANES = 8
LANES = 128
BF16_ROWS = 16
MXU_COLS = 256
VMEM_LIMIT_BYTES = 58 * 1024 * 1024

TOKEN_TILE = 1024
FF_CHUNK = 512
CONV_HALO = 32
CONV_PHASES = 4
CONV_CHUNK = 32
CONV_BF16_TAPS = 8
CONV_TILE = 1024
CONV_SPLIT = 4


def _rms(x, g):
    return x * lax.rsqrt(jnp.mean(x * x, axis=-1, keepdims=True) + NORM_EPS) * g


def _sigmoid(x):
    return 1.0 / (1.0 + jnp.exp(-x))


def _interleave(main, side):
    n, m, j = len(main), len(side), 0
    for i, step in enumerate(main):
        step()
        while j < m and (j + 1) * n <= (i + 1) * m:
            side[j]()
            j += 1
    for step in side[j:]:
        step()


def _const_spec(shape):
    n = len(shape)
    return pl.BlockSpec(shape, lambda i: (0,) * n, pipeline_mode=pl.Buffered(1))


def _layer_spec(shape, layer):
    n = len(shape)
    return pl.BlockSpec((None,) + tuple(shape), lambda i: (layer,) + (0,) * n,
                        pipeline_mode=pl.Buffered(1))


def _compiler_params():
    return pltpu.CompilerParams(dimension_semantics=("arbitrary",),
                                vmem_limit_bytes=VMEM_LIMIT_BYTES)


def _mlp_body(x_ref, g_ref, wu_ref, wd_ref, o_ref):
    x = x_ref[...]
    h = _rms(x, g_ref[...]).astype(BF16)
    acc = x
    for c in range(wu_ref.shape[1] // FF_CHUNK):
        cols = slice(c * FF_CHUNK, (c + 1) * FF_CHUNK)
        hid = jnp.dot(h, wu_ref[:, cols].astype(BF16), preferred_element_type=F32)
        hid = jnp.square(jnp.maximum(hid, 0.0)).astype(BF16)
        acc = acc + jnp.dot(hid, wd_ref[cols, :].astype(BF16), preferred_element_type=F32)
    o_ref[...] = acc


def _mlp(x, g, w_up, w_down, layer):
    t, d = x.shape
    f = w_up.shape[2]
    return pl.pallas_call(
        _mlp_body,
        grid=(t // TOKEN_TILE,),
        in_specs=[
            pl.BlockSpec((TOKEN_TILE, d), lambda i: (i, 0)),
            _layer_spec((1, d), layer),
            _layer_spec((d, f), layer),
            _layer_spec((f, d), layer),
        ],
        out_specs=pl.BlockSpec((TOKEN_TILE, d), lambda i: (i, 0)),
        out_shape=jax.ShapeDtypeStruct((t, d), F32),
        compiler_params=_compiler_params(),
        name="mlp_block",
    )(x, g.reshape(g.shape[0], 1, d), w_up, w_down)


def _conv_body(x_ref, ng_ref, win_f32, bin_ref, dw_ref, dwb_ref, lng_ref, lnb_ref,
               wout_f32, bout_ref, o_ref, win_ref, wout_ref, dw16_ref, *bufs, tiles_per_seq):
    tm, d = x_ref.shape
    nslab = d // LANES
    part = tm // CONV_SPLIT
    span = CONV_PHASES * CONV_CHUNK
    first = (pl.program_id(0) % tiles_per_seq) == 0
    ubufs, cbufs = bufs[:CONV_SPLIT], bufs[CONV_SPLIT:]

    @pl.when(pl.program_id(0) == 0)
    def _():
        win_ref[...] = win_f32[...].astype(BF16)
        wout_ref[...] = wout_f32[...].astype(BF16)
        for k in range(CONV_WIDTH):
            dw16_ref[k * BF16_ROWS:(k + 1) * BF16_ROWS, :] = jnp.broadcast_to(
                dw_ref[k:k + 1, :], (BF16_ROWS, d)).astype(BF16)

    @pl.when(first)
    def _():
        ubufs[0][:, 0:CONV_HALO, :] = jnp.zeros((nslab, CONV_HALO, LANES), F32)

    @pl.when(jnp.logical_not(first))
    def _():
        ubufs[0][:, 0:CONV_HALO, :] = ubufs[-1][:, part:part + CONV_HALO, :]

    def glu_steps(gi):
        rows = slice(gi * part, (gi + 1) * part)
        h = _rms(x_ref[rows, :], ng_ref[...]).astype(BF16)

        def step(c0):
            def run():
                a_cols = slice(c0, c0 + MXU_COLS)
                g_cols = slice(d + c0, d + c0 + MXU_COLS)
                a = jnp.dot(h, win_ref[:, a_cols], preferred_element_type=F32) + bin_ref[:, a_cols]
                g = jnp.dot(h, win_ref[:, g_cols], preferred_element_type=F32) + bin_ref[:, g_cols]
                glu = a * _sigmoid(g)
                for i in range(MXU_COLS // LANES):
                    piece = glu[:, i * LANES:(i + 1) * LANES]
                    ubufs[gi][c0 // LANES + i, CONV_HALO:CONV_HALO + part, :] = piece
                    if gi + 1 < CONV_SPLIT:
                        ubufs[gi + 1][c0 // LANES + i, 0:CONV_HALO, :] = piece[part - CONV_HALO:, :]
            return run
        return [step(c0) for c0 in range(0, d, MXU_COLS)]

    def conv_steps(gi):
        def step(s):
            def run():
                lanes = slice(s * LANES, (s + 1) * LANES)
                def w16(k):
                    w = dw16_ref[k * BF16_ROWS:(k + 1) * BF16_ROWS, lanes]
                    return jnp.concatenate([w] * (CONV_CHUNK // BF16_ROWS), axis=0)

                for t0 in range(0, part, span):
                    acc = [jnp.zeros((CONV_CHUNK, LANES), F32) + dwb_ref[:, lanes]] * CONV_PHASES
                    run16 = [None] * CONV_PHASES
                    for c in range(-(CONV_WIDTH - 1), CONV_PHASES):
                        lc = ubufs[gi][s, pl.ds(CONV_HALO + t0 + c, CONV_CHUNK, stride=CONV_PHASES), :]
                        lc = lc.astype(BF16)
                        for ph in range(CONV_PHASES):
                            k = c - ph + CONV_WIDTH - 1
                            if 0 <= k < CONV_WIDTH:
                                prod = lc * w16(k)
                                run16[ph] = prod if run16[ph] is None else run16[ph] + prod
                                if (k + 1) % CONV_BF16_TAPS == 0 or k == CONV_WIDTH - 1:
                                    acc[ph] = acc[ph] + run16[ph].astype(F32)
                                    run16[ph] = None
                    for ph in range(CONV_PHASES):
                        cbufs[gi][s, pl.ds(t0 + ph, CONV_CHUNK, stride=CONV_PHASES), :] = acc[ph]
            return run
        return [step(s) for s in range(nslab)]

    def out_steps(gi):
        rows = slice(gi * part, (gi + 1) * part)
        c = jnp.concatenate([cbufs[gi][s] for s in range(nslab)], axis=1)
        mu = jnp.mean(c, axis=-1, keepdims=True)
        cc = c - mu
        y = cc * lax.rsqrt(jnp.mean(cc * cc, axis=-1, keepdims=True) + NORM_EPS)
        y = y * lng_ref[...] + lnb_ref[...]
        y = (y * _sigmoid(y)).astype(BF16)

        def step(c0):
            def run():
                cols = slice(c0, c0 + MXU_COLS)
                o_ref[rows, cols] = (x_ref[rows, cols] + bout_ref[:, cols]
                                     + jnp.dot(y, wout_ref[:, cols], preferred_element_type=F32))
            return run
        return [step(c0) for c0 in range(0, d, MXU_COLS)]

    _interleave(glu_steps(0), [])
    for gi in range(CONV_SPLIT):
        mxu_side = []
        if gi > 0:
            mxu_side += out_steps(gi - 1)
        if gi + 1 < CONV_SPLIT:
            mxu_side += glu_steps(gi + 1)
        _interleave(conv_steps(gi), mxu_side)
    _interleave(out_steps(CONV_SPLIT - 1), [])


def _conv_block(x, seq, layer, ng, w_in, b_in, dw, dw_b, ln_g, ln_b, w_out, b_out):
    t, d = x.shape
    row = lambda v: v.reshape(1, -1)
    body = functools.partial(_conv_body, tiles_per_seq=seq // CONV_TILE)
    part = CONV_TILE // CONV_SPLIT
    return pl.pallas_call(
        body,
        grid=(t // CONV_TILE,),
        in_specs=[
            pl.BlockSpec((CONV_TILE, d), lambda i: (i, 0)),
            _const_spec((1, d)),
            _layer_spec((d, 2 * d), layer),
            _const_spec((1, 2 * d)),
            _const_spec((CONV_WIDTH, d)),
            _const_spec((1, d)),
            _const_spec((1, d)),
            _const_spec((1, d)),
            _layer_spec((d, d), layer),
            _const_spec((1, d)),
        ],
        out_specs=pl.BlockSpec((CONV_TILE, d), lambda i: (i, 0)),
        out_shape=jax.ShapeDtypeStruct((t, d), F32),
        scratch_shapes=(
            [pltpu.VMEM((d, 2 * d), BF16), pltpu.VMEM((d, d), BF16),
             pltpu.VMEM((CONV_WIDTH * BF16_ROWS, d), BF16)]
            + [pltpu.VMEM((d // LANES, part + CONV_HALO, LANES), F32)] * CONV_SPLIT
            + [pltpu.VMEM((d // LANES, part, LANES), F32)] * CONV_SPLIT),
        compiler_params=_compiler_params(),
        name="conv_block",
    )(x, row(ng), w_in, row(b_in), dw, row(dw_b), row(ln_g), row(ln_b), w_out, row(b_out))


def _t5_causal_bucket(dist):
    n = np.maximum(dist, 0)
    max_exact = REL_BUCKETS // 2
    large = max_exact + (np.log(np.maximum(n, 1).astype(np.float32) / max_exact)
                         / math.log(REL_MAX_DIST / max_exact)
                         * (REL_BUCKETS - max_exact)).astype(np.int32)
    large = np.minimum(large, REL_BUCKETS - 1)
    return np.where(n < max_exact, n, large).astype(np.int32)


def _bias_body(rel_ref, bucket_ref, o_ref):
    g = pl.program_id(0)
    bucket = bucket_ref[...]
    for hh in range(GROUP):
        acc = jnp.full(bucket.shape, NEG_INF, F32)
        for b in range(REL_BUCKETS):
            acc = jnp.where(bucket == b, rel_ref[b, g * GROUP + hh] * LOG2E, acc)
        o_ref[0, :, hh * WINDOW:(hh + 1) * WINDOW] = acc


def _band_bias(rel_bias):
    k_loc = np.arange(2 * WINDOW)[:, None]
    q_loc = np.arange(WINDOW)[None, :]
    dist = q_loc + WINDOW - k_loc
    bucket = np.where((dist >= 0) & (dist < WINDOW), _t5_causal_bucket(dist), -1)
    return pl.pallas_call(
        _bias_body,
        grid=(N_KV_HEADS,),
        in_specs=[
            pl.BlockSpec(memory_space=pltpu.SMEM),
            pl.BlockSpec((2 * WINDOW, WINDOW), lambda i: (0, 0)),
        ],
        out_specs=pl.BlockSpec((1, 2 * WINDOW, GROUP * WINDOW), lambda i: (i, 0, 0)),
        out_shape=jax.ShapeDtypeStruct((N_KV_HEADS, 2 * WINDOW, GROUP * WINDOW), F32),
        name="band_bias",
    )(rel_bias, jnp.asarray(bucket.astype(np.int32)))


def _attn_body(x_ref, ng_ref, wqkv_f32, bqkv_ref, kgain_ref, bias_ref, sinks_ref, wo_f32,
               bo_ref, o_ref, wqkv_ref, wo_ref, qbuf, kbuf, vbuf, abuf, *sbufs, tiles_per_seq):
    tq, d = x_ref.shape
    kvd = N_KV_HEADS * HEAD_DIM
    nblk = tq // WINDOW
    first = (pl.program_id(0) % tiles_per_seq) == 0

    @pl.when(pl.program_id(0) == 0)
    def _():
        wqkv_ref[...] = wqkv_f32[...].T.astype(BF16)
        wo_ref[...] = wo_f32[...].T.astype(BF16)

    @pl.when(first)
    def _():
        kbuf[0:WINDOW, :] = jnp.zeros((WINDOW, kvd), BF16)
        vbuf[:, 0:WINDOW] = jnp.zeros((kvd, WINDOW), BF16)

    @pl.when(jnp.logical_not(first))
    def _():
        kbuf[0:WINDOW, :] = kbuf[tq:tq + WINDOW, :]
        vbuf[:, 0:WINDOW] = vbuf[:, tq:tq + WINDOW]

    x = x_ref[...]
    h = _rms(x, ng_ref[...]).astype(BF16)
    qkv = lax.dot_general(wqkv_ref[...], h, (((1,), (1,)), ((), ())),
                          preferred_element_type=F32)
    bcol = bqkv_ref[...]

    def head_rows(r0):
        rows = slice(r0, r0 + HEAD_DIM)
        return qkv[rows, :] + jnp.concatenate([bcol[rows, :]] * nblk, axis=1)

    for hd in range(N_HEADS):
        qh = head_rows(hd * HEAD_DIM)
        ms = jnp.mean(qh * qh, axis=0, keepdims=True)
        qbuf[hd * HEAD_DIM:(hd + 1) * HEAD_DIM, :] = (qh * lax.rsqrt(ms + NORM_EPS)).astype(BF16)
    kn = []
    for g in range(N_KV_HEADS):
        kh = head_rows(d + g * HEAD_DIM)
        ms = jnp.mean(kh * kh, axis=0, keepdims=True)
        kn.append(kh * lax.rsqrt(ms + NORM_EPS))
        vbuf[g * HEAD_DIM:(g + 1) * HEAD_DIM, WINDOW:WINDOW + tq] = (
            head_rows(d + kvd + g * HEAD_DIM).astype(BF16))
    kt = jnp.concatenate(kn, axis=0)
    kbuf[WINDOW:WINDOW + tq, :] = (kt.T * kgain_ref[...]).astype(BF16)

    kk = lax.broadcasted_iota(jnp.int32, (2 * WINDOW, WINDOW), 0)
    first_key = jnp.where(first, WINDOW, 0)
    ones_rows = (lax.broadcasted_iota(jnp.int32, (BF16_ROWS, 2 * WINDOW), 0) == 0).astype(BF16)
    pairs = [(j, g) for j in range(nblk) for g in range(N_KV_HEADS)]

    def score_steps(n):
        j, g = pairs[n]
        kw = kbuf[j * WINDOW:(j + 2) * WINDOW, g * HEAD_DIM:(g + 1) * HEAD_DIM]

        def step(c0):
            def run():
                qt = jnp.concatenate(
                    [qbuf[(g * GROUP + c0 // WINDOW + i) * HEAD_DIM:
                          (g * GROUP + c0 // WINDOW + i + 1) * HEAD_DIM, j * WINDOW:(j + 1) * WINDOW]
                     for i in range(MXU_COLS // WINDOW)], axis=1)
                sbufs[n % 2][:, c0:c0 + MXU_COLS] = jnp.dot(kw, qt, preferred_element_type=F32)
            return run
        return [step(c0) for c0 in range(0, GROUP * WINDOW, MXU_COLS)]

    def softmax_steps(n):
        j, g = pairs[n]
        cols = slice(j * WINDOW, (j + 1) * WINDOW)
        band = slice(j * WINDOW, (j + 2) * WINDOW)
        va = jnp.concatenate([vbuf[g * HEAD_DIM:(g + 1) * HEAD_DIM, band], ones_rows], axis=0)

        def step(c0):
            def run():
                ps, ms = [], []
                for i in range(MXU_COLS // WINDOW):
                    blk = slice(c0 + i * WINDOW, c0 + (i + 1) * WINDOW)
                    s = sbufs[n % 2][:, blk] + bias_ref[g, :, blk]
                    if j == 0:
                        s = jnp.where(kk >= first_key, s, NEG_INF)
                    sink = sinks_ref[g * GROUP + c0 // WINDOW + i] * LOG2E
                    m = jnp.maximum(jnp.max(s, axis=0, keepdims=True), sink)
                    ps.append(jnp.exp2(s - m).astype(BF16))
                    ms.append((m, sink))
                ot = jnp.dot(va, jnp.concatenate(ps, axis=1), preferred_element_type=F32)
                for i, (m, sink) in enumerate(ms):
                    hd = g * GROUP + c0 // WINDOW + i
                    denom = ot[HEAD_DIM:HEAD_DIM + 1, i * WINDOW:(i + 1) * WINDOW] + jnp.exp2(sink - m)
                    abuf[hd * HEAD_DIM:(hd + 1) * HEAD_DIM, cols] = (
                        ot[0:HEAD_DIM, i * WINDOW:(i + 1) * WINDOW] * (1.0 / denom)).astype(BF16)
            return run
        return [step(c0) for c0 in range(0, GROUP * WINDOW, MXU_COLS)]

    _interleave(score_steps(0), [])
    for n in range(len(pairs)):
        _interleave(softmax_steps(n), score_steps(n + 1) if n + 1 < len(pairs) else [])

    yt = jnp.dot(wo_ref[...], abuf[...], preferred_element_type=F32)
    o_ref[...] = x + yt.T + bo_ref[...]


def _attn_block(x, seq, layer, ng, w_qkv, b_qkv, q_g, k_g, sinks, w_o, b_o, bias):
    t, d = x.shape
    qkv_dim = w_qkv.shape[2]
    kvd = N_KV_HEADS * HEAD_DIM
    row = lambda v: v.reshape(1, -1)
    kgain = jnp.tile(q_g * k_g * (LOG2E / math.sqrt(HEAD_DIM)), N_KV_HEADS)
    body = functools.partial(_attn_body, tiles_per_seq=seq // TOKEN_TILE)
    return pl.pallas_call(
        body,
        grid=(t // TOKEN_TILE,),
        in_specs=[
            pl.BlockSpec((TOKEN_TILE, d), lambda i: (i, 0)),
            _const_spec((1, d)),
            _layer_spec((d, qkv_dim), layer),
            _const_spec((qkv_dim, LANES)),
            _const_spec((1, kvd)),
            _const_spec((N_KV_HEADS, 2 * WINDOW, GROUP * WINDOW)),
            pl.BlockSpec(memory_space=pltpu.SMEM),
            _layer_spec((d, d), layer),
            _const_spec((1, d)),
        ],
        out_specs=pl.BlockSpec((TOKEN_TILE, d), lambda i: (i, 0)),
        out_shape=jax.ShapeDtypeStruct((t, d), F32),
        scratch_shapes=[
            pltpu.VMEM((qkv_dim, d), BF16),
            pltpu.VMEM((d, d), BF16),
            pltpu.VMEM((d, TOKEN_TILE), BF16),
            pltpu.VMEM((TOKEN_TILE + WINDOW, kvd), BF16),
            pltpu.VMEM((kvd, TOKEN_TILE + WINDOW), BF16),
            pltpu.VMEM((d, TOKEN_TILE), BF16),
            pltpu.VMEM((2 * WINDOW, GROUP * WINDOW), F32),
            pltpu.VMEM((2 * WINDOW, GROUP * WINDOW), F32),
        ],
        compiler_params=_compiler_params(),
        name="attn_block",
    )(x, row(ng), w_qkv, jnp.broadcast_to(b_qkv[:, None], (qkv_dim, LANES)),
      row(kgain), bias, sinks, w_o, row(b_o))


def kernel(x, conv_norm_g, conv_w_in, conv_b_in, conv_dw, conv_dw_b, conv_ln_g, conv_ln_b, conv_w_out, conv_b_out, attn_norm_g, w_qkv, b_qkv, q_norm_g, k_norm_g, sinks, w_o, b_o, rel_bias, mlp_norm_g, w_up, w_down):
    b, s, d = x.shape
    assert s % TOKEN_TILE == 0 and TOKEN_TILE % WINDOW == 0 and d == N_HEADS * HEAD_DIM
    depth = mlp_norm_g.shape[0]
    bias = _band_bias(rel_bias)
    xf = x.reshape(b * s, d)
    for i in range(depth):
        j = i // 2
        if i % 2 == 0:
            xf = _conv_block(xf, s, j, conv_norm_g[j], conv_w_in, conv_b_in[j], conv_dw[j],
                             conv_dw_b[j], conv_ln_g[j], conv_ln_b[j], conv_w_out,
                             conv_b_out[j])
        else:
            xf = _attn_block(xf, s, j, attn_norm_g[j], w_qkv, b_qkv[j], q_norm_g[j],
                             k_norm_g[j], sinks[j], w_o, b_o[j], bias)
        xf = _mlp(xf, mlp_norm_g, w_up, w_down, i)
    return xf.reshape(b, s, d)
```

```python
import functools
import math

import numpy as np
import jax
import jax.numpy as jnp
from jax import lax
from jax.experimental import pallas as pl
from jax.experimental.pallas import tpu as pltpu

F32 = jnp.float32
BF16 = jnp.bfloat16

CONV_WIDTH = 31
N_HEADS = 16
N_KV_HEADS = 2
HEAD_DIM = 64
GROUP = N_HEADS // N_KV_HEADS
WINDOW = 128
REL_BUCKETS = 32
REL_MAX_DIST = 128
NORM_EPS = 1e-6
NEG_INF = -1e30
LOG2E = math.log2(math.e)

SUBLANES = 8
LANES = 128
BF16_ROWS = 16
MXU_COLS = 256
VMEM_LIMIT_BYTES = 58 * 1024 * 1024

TOKEN_TILE = 1024
FF_CHUNK = 512
CONV_HALO = 32
CONV_PHASES = 4
CONV_CHUNK = 32
CONV_BF16_TAPS = 8
CONV_TILE = 1024
CONV_SPLIT = 4


def _rms(x, g):
    return x * lax.rsqrt(jnp.mean(x * x, axis=-1, keepdims=True) + NORM_EPS) * g


def _sigmoid(x):
    return 1.0 / (1.0 + jnp.exp(-x))


def _interleave(main, side):
    n, m, j = len(main), len(side), 0
    for i, step in enumerate(main):
        step()
        while j < m and (j + 1) * n <= (i + 1) * m:
            side[j]()
            j += 1
    for step in side[j:]:
        step()


def _const_spec(shape):
    n = len(shape)
    return pl.BlockSpec(shape, lambda i: (0,) * n, pipeline_mode=pl.Buffered(1))


def _layer_spec(shape, layer):
    n = len(shape)
    return pl.BlockSpec((None,) + tuple(shape), lambda i: (layer,) + (0,) * n,
                        pipeline_mode=pl.Buffered(1))


def _compiler_params():
    return pltpu.CompilerParams(dimension_semantics=("arbitrary",),
                                vmem_limit_bytes=VMEM_LIMIT_BYTES)


def _mlp_body(x_ref, g_ref, wu_ref, wd_ref, o_ref):
    x = x_ref[...]
    h = _rms(x, g_ref[...]).astype(BF16)
    acc = x
    for c in range(wu_ref.shape[1] // FF_CHUNK):
        cols = slice(c * FF_CHUNK, (c + 1) * FF_CHUNK)
        hid = jnp.dot(h, wu_ref[:, cols].astype(BF16), preferred_element_type=F32)
        hid = jnp.square(jnp.maximum(hid, 0.0)).astype(BF16)
        acc = acc + jnp.dot(hid, wd_ref[cols, :].astype(BF16), preferred_element_type=F32)
    o_ref[...] = acc


def _mlp(x, g, w_up, w_down, layer):
    t, d = x.shape
    f = w_up.shape[2]
    return pl.pallas_call(
        _mlp_body,
        grid=(t // TOKEN_TILE,),
        in_specs=[
            pl.BlockSpec((TOKEN_TILE, d), lambda i: (i, 0)),
            _layer_spec((1, d), layer),
            _layer_spec((d, f), layer),
            _layer_spec((f, d), layer),
        ],
        out_specs=pl.BlockSpec((TOKEN_TILE, d), lambda i: (i, 0)),
        out_shape=jax.ShapeDtypeStruct((t, d), F32),
        compiler_params=_compiler_params(),
        name="mlp_block",
    )(x, g.reshape(g.shape[0], 1, d), w_up, w_down)


def _conv_body(x_ref, ng_ref, win_f32, bin_ref, dw_ref, dwb_ref, lng_ref, lnb_ref,
               wout_f32, bout_ref, o_ref, win_ref, wout_ref, dw16_ref, *bufs, tiles_per_seq):
    tm, d = x_ref.shape
    nslab = d // LANES
    part = tm // CONV_SPLIT
    span = CONV_PHASES * CONV_CHUNK
    first = (pl.program_id(0) % tiles_per_seq) == 0
    ubufs, cbufs = bufs[:CONV_SPLIT], bufs[CONV_SPLIT:]

    @pl.when(pl.program_id(0) == 0)
    def _():
        win_ref[...] = win_f32[...].astype(BF16)
        wout_ref[...] = wout_f32[...].astype(BF16)
        for k in range(CONV_WIDTH):
            dw16_ref[k * BF16_ROWS:(k + 1) * BF16_ROWS, :] = jnp.broadcast_to(
                dw_ref[k:k + 1, :], (BF16_ROWS, d)).astype(BF16)

    @pl.when(first)
    def _():
        ubufs[0][:, 0:CONV_HALO, :] = jnp.zeros((nslab, CONV_HALO, LANES), F32)

    @pl.when(jnp.logical_not(first))
    def _():
        ubufs[0][:, 0:CONV_HALO, :] = ubufs[-1][:, part:part + CONV_HALO, :]

    def glu_steps(gi):
        rows = slice(gi * part, (gi + 1) * part)
        h = _rms(x_ref[rows, :], ng_ref[...]).astype(BF16)

        def step(c0):
            def run():
                a_cols = slice(c0, c0 + MXU_COLS)
                g_cols = slice(d + c0, d + c0 + MXU_COLS)
                a = jnp.dot(h, win_ref[:, a_cols], preferred_element_type=F32) + bin_ref[:, a_cols]
                g = jnp.dot(h, win_ref[:, g_cols], preferred_element_type=F32) + bin_ref[:, g_cols]
                glu = a * _sigmoid(g)
                for i in range(MXU_COLS // LANES):
                    piece = glu[:, i * LANES:(i + 1) * LANES]
                    ubufs[gi][c0 // LANES + i, CONV_HALO:CONV_HALO + part, :] = piece
                    if gi + 1 < CONV_SPLIT:
                        ubufs[gi + 1][c0 // LANES + i, 0:CONV_HALO, :] = piece[part - CONV_HALO:, :]
            return run
        return [step(c0) for c0 in range(0, d, MXU_COLS)]

    def conv_steps(gi):
        def step(s):
            def run():
                lanes = slice(s * LANES, (s + 1) * LANES)
                def w16(k):
                    w = dw16_ref[k * BF16_ROWS:(k + 1) * BF16_ROWS, lanes]
                    return jnp.concatenate([w] * (CONV_CHUNK // BF16_ROWS), axis=0)

                for t0 in range(0, part, span):
                    acc = [jnp.zeros((CONV_CHUNK, LANES), F32) + dwb_ref[:, lanes]] * CONV_PHASES
                    run16 = [None] * CONV_PHASES
                    operand = {}
                    for k in range(CONV_WIDTH):
                        wk = w16(k)
                        for ph in range(CONV_PHASES):
                            c = ph + k - (CONV_WIDTH - 1)
                            if c not in operand:
                                operand[c] = ubufs[gi][s, pl.ds(CONV_HALO + t0 + c, CONV_CHUNK,
                                                                stride=CONV_PHASES), :].astype(BF16)
                            prod = operand[c] * wk
                            run16[ph] = prod if run16[ph] is None else run16[ph] + prod
                            if (k + 1) % CONV_BF16_TAPS == 0 or k == CONV_WIDTH - 1:
                                acc[ph] = acc[ph] + run16[ph].astype(F32)
                                run16[ph] = None
                        operand.pop(k - (CONV_WIDTH - 1), None)
                    for ph in range(CONV_PHASES):
                        cbufs[gi][s, pl.ds(t0 + ph, CONV_CHUNK, stride=CONV_PHASES), :] = acc[ph]
            return run
        return [step(s) for s in range(nslab)]

    def out_steps(gi):
        rows = slice(gi * part, (gi + 1) * part)
        c = jnp.concatenate([cbufs[gi][s] for s in range(nslab)], axis=1)
        mu = jnp.mean(c, axis=-1, keepdims=True)
        cc = c - mu
        y = cc * lax.rsqrt(jnp.mean(cc * cc, axis=-1, keepdims=True) + NORM_EPS)
        y = y * lng_ref[...] + lnb_ref[...]
        y = (y * _sigmoid(y)).astype(BF16)

        def step(c0):
            def run():
                cols = slice(c0, c0 + MXU_COLS)
                o_ref[rows, cols] = (x_ref[rows, cols] + bout_ref[:, cols]
                                     + jnp.dot(y, wout_ref[:, cols], preferred_element_type=F32))
            return run
        return [step(c0) for c0 in range(0, d, MXU_COLS)]

    _interleave(glu_steps(0), [])
    for gi in range(CONV_SPLIT):
        mxu_side = []
        if gi > 0:
            mxu_side += out_steps(gi - 1)
        if gi + 1 < CONV_SPLIT:
            mxu_side += glu_steps(gi + 1)
        _interleave(conv_steps(gi), mxu_side)
    _interleave(out_steps(CONV_SPLIT - 1), [])


def _conv_block(x, seq, layer, ng, w_in, b_in, dw, dw_b, ln_g, ln_b, w_out, b_out):
    t, d = x.shape
    row = lambda v: v.reshape(1, -1)
    body = functools.partial(_conv_body, tiles_per_seq=seq // CONV_TILE)
    part = CONV_TILE // CONV_SPLIT
    return pl.pallas_call(
        body,
        grid=(t // CONV_TILE,),
        in_specs=[
            pl.BlockSpec((CONV_TILE, d), lambda i: (i, 0)),
            _const_spec((1, d)),
            _layer_spec((d, 2 * d), layer),
            _const_spec((1, 2 * d)),
            _const_spec((CONV_WIDTH, d)),
            _const_spec((1, d)),
            _const_spec((1, d)),
            _const_spec((1, d)),
            _layer_spec((d, d), layer),
            _const_spec((1, d)),
        ],
        out_specs=pl.BlockSpec((CONV_TILE, d), lambda i: (i, 0)),
        out_shape=jax.ShapeDtypeStruct((t, d), F32),
        scratch_shapes=(
            [pltpu.VMEM((d, 2 * d), BF16), pltpu.VMEM((d, d), BF16),
             pltpu.VMEM((CONV_WIDTH * BF16_ROWS, d), BF16)]
            + [pltpu.VMEM((d // LANES, part + CONV_HALO, LANES), F32)] * CONV_SPLIT
            + [pltpu.VMEM((d // LANES, part, LANES), F32)] * CONV_SPLIT),
        compiler_params=_compiler_params(),
        name="conv_block",
    )(x, row(ng), w_in, row(b_in), dw, row(dw_b), row(ln_g), row(ln_b), w_out, row(b_out))


def _t5_causal_bucket(dist):
    n = np.maximum(dist, 0)
    max_exact = REL_BUCKETS // 2
    large = max_exact + (np.log(np.maximum(n, 1).astype(np.float32) / max_exact)
                         / math.log(REL_MAX_DIST / max_exact)
                         * (REL_BUCKETS - max_exact)).astype(np.int32)
    large = np.minimum(large, REL_BUCKETS - 1)
    return np.where(n < max_exact, n, large).astype(np.int32)


def _bias_body(rel_ref, bucket_ref, o_ref):
    g = pl.program_id(0)
    bucket = bucket_ref[...]
    acc = [jnp.full(bucket.shape, NEG_INF, F32)] * GROUP
    for b in range(REL_BUCKETS):
        hit = bucket == b
        acc = [jnp.where(hit, rel_ref[b, g * GROUP + hh] * LOG2E, acc[hh]) for hh in range(GROUP)]
    for hh in range(GROUP):
        o_ref[0, :, hh * WINDOW:(hh + 1) * WINDOW] = acc[hh]


def _band_bias(rel_bias):
    k_loc = np.arange(2 * WINDOW)[:, None]
    q_loc = np.arange(WINDOW)[None, :]
    dist = q_loc + WINDOW - k_loc
    bucket = np.where((dist >= 0) & (dist < WINDOW), _t5_causal_bucket(dist), -1)
    return pl.pallas_call(
        _bias_body,
        grid=(N_KV_HEADS,),
        in_specs=[
            pl.BlockSpec(memory_space=pltpu.SMEM),
            pl.BlockSpec((2 * WINDOW, WINDOW), lambda i: (0, 0)),
        ],
        out_specs=pl.BlockSpec((1, 2 * WINDOW, GROUP * WINDOW), lambda i: (i, 0, 0)),
        out_shape=jax.ShapeDtypeStruct((N_KV_HEADS, 2 * WINDOW, GROUP * WINDOW), F32),
        name="band_bias",
    )(rel_bias, jnp.asarray(bucket.astype(np.int32)))


def _attn_body(x_ref, ng_ref, wqkv_f32, bqkv_ref, kgain_ref, bias_ref, sinks_ref, wo_f32,
               bo_ref, o_ref, wqkv_ref, wo_ref, qbuf, kbuf, vbuf, abuf, pbuf, *sbufs,
               tiles_per_seq):
    tq, d = x_ref.shape
    kvd = N_KV_HEADS * HEAD_DIM
    nblk = tq // WINDOW
    first = (pl.program_id(0) % tiles_per_seq) == 0

    @pl.when(pl.program_id(0) == 0)
    def _():
        wqkv_ref[...] = wqkv_f32[...].T.astype(BF16)
        wo_ref[...] = wo_f32[...].T.astype(BF16)

    @pl.when(first)
    def _():
        kbuf[0:WINDOW, :] = jnp.zeros((WINDOW, kvd), BF16)
        vbuf[:, 0:WINDOW] = jnp.zeros((kvd, WINDOW), BF16)

    @pl.when(jnp.logical_not(first))
    def _():
        kbuf[0:WINDOW, :] = kbuf[tq:tq + WINDOW, :]
        vbuf[:, 0:WINDOW] = vbuf[:, tq:tq + WINDOW]

    h = _rms(x_ref[...], ng_ref[...]).astype(BF16)

    def proj_step(r0):
        def run():
            pbuf[r0:r0 + MXU_COLS, :] = lax.dot_general(
                wqkv_ref[r0:r0 + MXU_COLS, :], h, (((1,), (1,)), ((), ())),
                preferred_element_type=F32)
        return run

    def head_rows(r0):
        rows = slice(r0, r0 + HEAD_DIM)
        return pbuf[rows, :] + jnp.concatenate([bqkv_ref[rows, :]] * nblk, axis=1)

    def q_norm_step(hd):
        def run():
            qh = head_rows(hd * HEAD_DIM)
            ms = jnp.mean(qh * qh, axis=0, keepdims=True)
            qbuf[hd * HEAD_DIM:(hd + 1) * HEAD_DIM, :] = (
                qh * lax.rsqrt(ms + NORM_EPS)).astype(BF16)
        return run

    def kv_step():
        kn = []
        for g in range(N_KV_HEADS):
            kh = head_rows(d + g * HEAD_DIM)
            ms = jnp.mean(kh * kh, axis=0, keepdims=True)
            kn.append(kh * lax.rsqrt(ms + NORM_EPS))
            vbuf[g * HEAD_DIM:(g + 1) * HEAD_DIM, WINDOW:WINDOW + tq] = (
                head_rows(d + kvd + g * HEAD_DIM).astype(BF16))
        kt = jnp.concatenate(kn, axis=0)
        kbuf[WINDOW:WINDOW + tq, :] = (kt.T * kgain_ref[...]).astype(BF16)

    heads_per_chunk = MXU_COLS // HEAD_DIM
    proj = [proj_step(r0) for r0 in range(0, d + 2 * kvd, MXU_COLS)]
    proj[0]()
    for i in range(d // MXU_COLS):
        proj[i + 1]()
        for hd in range(i * heads_per_chunk, (i + 1) * heads_per_chunk):
            q_norm_step(hd)()
    kv_step()

    kk = lax.broadcasted_iota(jnp.int32, (2 * WINDOW, WINDOW), 0)
    first_key = jnp.where(first, WINDOW, 0)
    ones_rows = (lax.broadcasted_iota(jnp.int32, (BF16_ROWS, 2 * WINDOW), 0) == 0).astype(BF16)
    pairs = [(j, g) for j in range(nblk) for g in range(N_KV_HEADS)]

    def score_steps(n):
        j, g = pairs[n]
        kw = kbuf[j * WINDOW:(j + 2) * WINDOW, g * HEAD_DIM:(g + 1) * HEAD_DIM]

        def step(c0):
            def run():
                qt = jnp.concatenate(
                    [qbuf[(g * GROUP + c0 // WINDOW + i) * HEAD_DIM:
                          (g * GROUP + c0 // WINDOW + i + 1) * HEAD_DIM, j * WINDOW:(j + 1) * WINDOW]
                     for i in range(MXU_COLS // WINDOW)], axis=1)
                sbufs[n % 2][:, c0:c0 + MXU_COLS] = jnp.dot(kw, qt, preferred_element_type=F32)
            return run
        return [step(c0) for c0 in range(0, GROUP * WINDOW, MXU_COLS)]

    def softmax_steps(n):
        j, g = pairs[n]
        cols = slice(j * WINDOW, (j + 1) * WINDOW)
        band = slice(j * WINDOW, (j + 2) * WINDOW)
        va = jnp.concatenate([vbuf[g * HEAD_DIM:(g + 1) * HEAD_DIM, band], ones_rows], axis=0)

        def step(c0):
            def run():
                ps, ms = [], []
                for i in range(MXU_COLS // WINDOW):
                    blk = slice(c0 + i * WINDOW, c0 + (i + 1) * WINDOW)
                    s = sbufs[n % 2][:, blk] + bias_ref[g, :, blk]
                    if j == 0:
                        s = jnp.where(kk >= first_key, s, NEG_INF)
                    sink = sinks_ref[g * GROUP + c0 // WINDOW + i] * LOG2E
                    m = jnp.maximum(jnp.max(s, axis=0, keepdims=True), sink)
                    ps.append(jnp.exp2(s - m).astype(BF16))
                    ms.append((m, sink))
                ot = jnp.dot(va, jnp.concatenate(ps, axis=1), preferred_element_type=F32)
                for i, (m, sink) in enumerate(ms):
                    hd = g * GROUP + c0 // WINDOW + i
                    denom = ot[HEAD_DIM:HEAD_DIM + 1, i * WINDOW:(i + 1) * WINDOW] + jnp.exp2(sink - m)
                    abuf[hd * HEAD_DIM:(hd + 1) * HEAD_DIM, cols] = (
                        ot[0:HEAD_DIM, i * WINDOW:(i + 1) * WINDOW] * (1.0 / denom)).astype(BF16)
            return run
        return [step(c0) for c0 in range(0, GROUP * WINDOW, MXU_COLS)]

    _interleave(score_steps(0), [])
    for n in range(len(pairs)):
        _interleave(softmax_steps(n), score_steps(n + 1) if n + 1 < len(pairs) else [])

    def out_proj_step(r0):
        def run():
            pbuf[r0:r0 + MXU_COLS, :] = jnp.dot(wo_ref[r0:r0 + MXU_COLS, :], abuf[...],
                                                preferred_element_type=F32)
        return run

    def residual_step(r0):
        def run():
            cols = slice(r0, r0 + MXU_COLS)
            o_ref[:, cols] = x_ref[:, cols] + pbuf[cols, :].T + bo_ref[:, cols]
        return run

    out_proj_step(0)()
    for r0 in range(0, d, MXU_COLS):
        if r0 + MXU_COLS < d:
            out_proj_step(r0 + MXU_COLS)()
        residual_step(r0)()


def _attn_block(x, seq, layer, ng, w_qkv, b_qkv, q_g, k_g, sinks, w_o, b_o, bias):
    t, d = x.shape
    qkv_dim = w_qkv.shape[2]
    kvd = N_KV_HEADS * HEAD_DIM
    row = lambda v: v.reshape(1, -1)
    kgain = jnp.tile(q_g * k_g * (LOG2E / math.sqrt(HEAD_DIM)), N_KV_HEADS)
    body = functools.partial(_attn_body, tiles_per_seq=seq // TOKEN_TILE)
    return pl.pallas_call(
        body,
        grid=(t // TOKEN_TILE,),
        in_specs=[
            pl.BlockSpec((TOKEN_TILE, d), lambda i: (i, 0)),
            _const_spec((1, d)),
            _layer_spec((d, qkv_dim), layer),
            _const_spec((qkv_dim, LANES)),
            _const_spec((1, kvd)),
            _const_spec((N_KV_HEADS, 2 * WINDOW, GROUP * WINDOW)),
            pl.BlockSpec(memory_space=pltpu.SMEM),
            _layer_spec((d, d), layer),
            _const_spec((1, d)),
        ],
        out_specs=pl.BlockSpec((TOKEN_TILE, d), lambda i: (i, 0)),
        out_shape=jax.ShapeDtypeStruct((t, d), F32),
        scratch_shapes=[
            pltpu.VMEM((qkv_dim, d), BF16),
            pltpu.VMEM((d, d), BF16),
            pltpu.VMEM((d, TOKEN_TILE), BF16),
            pltpu.VMEM((TOKEN_TILE + WINDOW, kvd), BF16),
            pltpu.VMEM((kvd, TOKEN_TILE + WINDOW), BF16),
            pltpu.VMEM((d, TOKEN_TILE), BF16),
            pltpu.VMEM((qkv_dim, TOKEN_TILE), F32),
            pltpu.VMEM((2 * WINDOW, GROUP * WINDOW), F32),
            pltpu.VMEM((2 * WINDOW, GROUP * WINDOW), F32),
        ],
        compiler_params=_compiler_params(),
        name="attn_block",
    )(x, row(ng), w_qkv, jnp.broadcast_to(b_qkv[:, None], (qkv_dim, LANES)),
      row(kgain), bias, sinks, w_o, row(b_o))


def kernel(x, conv_norm_g, conv_w_in, conv_b_in, conv_dw, conv_dw_b, conv_ln_g, conv_ln_b, conv_w_out, conv_b_out, attn_norm_g, w_qkv, b_qkv, q_norm_g, k_norm_g, sinks, w_o, b_o, rel_bias, mlp_norm_g, w_up, w_down):
    b, s, d = x.shape
    assert s % TOKEN_TILE == 0 and TOKEN_TILE % WINDOW == 0 and d == N_HEADS * HEAD_DIM
    depth = mlp_norm_g.shape[0]
    bias = _band_bias(rel_bias)
    xf = x.reshape(b * s, d)
    for i in range(depth):
        j = i // 2
        if i % 2 == 0:
            xf = _conv_block(xf, s, j, conv_norm_g[j], conv_w_in, conv_b_in[j], conv_dw[j],
                             conv_dw_b[j], conv_ln_g[j], conv_ln_b[j], conv_w_out,
                             conv_b_out[j])
        else:
            xf = _attn_block(xf, s, j, attn_norm_g[j], w_qkv, b_qkv[j], q_norm_g[j],
                             k_norm_g[j], sinks[j], w_o, b_o[j], bias)
        xf = _mlp(xf, mlp_norm_g, w_up, w_down, i)
    return xf.reshape(b, s, d)
```

```python
import functools
import math

import numpy as np
import jax
import jax.numpy as jnp
from jax import lax
from jax.experimental import pallas as pl
from jax.experimental.pallas import tpu as pltpu

F32 = jnp.float32
BF16 = jnp.bfloat16

CONV_WIDTH = 31
N_HEADS = 16
N_KV_HEADS = 2
HEAD_DIM = 64
GROUP = N_HEADS // N_KV_HEADS
WINDOW = 128
REL_BUCKETS = 32
REL_MAX_DIST = 128
NORM_EPS = 1e-6
NEG_INF = -1e30
LOG2E = math.log2(math.e)

SUBLANES = 8
LANES = 128
BF16_ROWS = 16
MXU_COLS = 256
VMEM_LIMIT_BYTES = 58 * 1024 * 1024

TOKEN_TILE = 1024
FF_CHUNK = 512
CONV_HALO = 32
CONV_PHASES = 4
CONV_CHUNK = 32
CONV_BF16_TAPS = 8
CONV_TILE = 1024
CONV_SPLIT = 4


def _rms(x, g):
    return x * lax.rsqrt(jnp.mean(x * x, axis=-1, keepdims=True) + NORM_EPS) * g


def _sigmoid(x):
    return 1.0 / (1.0 + jnp.exp2(x * -LOG2E))


def _interleave(main, side):
    n, m, j = len(main), len(side), 0
    for i, step in enumerate(main):
        step()
        while j < m and (j + 1) * n <= (i + 1) * m:
            side[j]()
            j += 1
    for step in side[j:]:
        step()


def _const_spec(shape):
    n = len(shape)
    return pl.BlockSpec(shape, lambda i: (0,) * n, pipeline_mode=pl.Buffered(1))


def _layer_spec(shape, layer):
    n = len(shape)
    return pl.BlockSpec((None,) + tuple(shape), lambda i: (layer,) + (0,) * n,
                        pipeline_mode=pl.Buffered(1))


def _compiler_params():
    return pltpu.CompilerParams(dimension_semantics=("arbitrary",),
                                vmem_limit_bytes=VMEM_LIMIT_BYTES)


def _mlp_body(x_ref, g_ref, wu_ref, wd_ref, o_ref):
    x = x_ref[...]
    h = _rms(x, g_ref[...]).astype(BF16)
    acc = x
    for c in range(wu_ref.shape[1] // FF_CHUNK):
        cols = slice(c * FF_CHUNK, (c + 1) * FF_CHUNK)
        hid = jnp.dot(h, wu_ref[:, cols].astype(BF16), preferred_element_type=F32)
        hid = jnp.square(jnp.maximum(hid, 0.0)).astype(BF16)
        acc = acc + jnp.dot(hid, wd_ref[cols, :].astype(BF16), preferred_element_type=F32)
    o_ref[...] = acc


def _mlp(x, g, w_up, w_down, layer):
    t, d = x.shape
    f = w_up.shape[2]
    return pl.pallas_call(
        _mlp_body,
        grid=(t // TOKEN_TILE,),
        in_specs=[
            pl.BlockSpec((TOKEN_TILE, d), lambda i: (i, 0)),
            _layer_spec((1, d), layer),
            _layer_spec((d, f), layer),
            _layer_spec((f, d), layer),
        ],
        out_specs=pl.BlockSpec((TOKEN_TILE, d), lambda i: (i, 0)),
        out_shape=jax.ShapeDtypeStruct((t, d), F32),
        compiler_params=_compiler_params(),
        name="mlp_block",
    )(x, g.reshape(g.shape[0], 1, d), w_up, w_down)


def _conv_body(x_ref, ng_ref, win_f32, bin_ref, dw_ref, dwb_ref, lng_ref, lnb_ref,
               wout_f32, bout_ref, o_ref, win_ref, wout_ref, dw16_ref, *bufs, tiles_per_seq):
    tm, d = x_ref.shape
    nslab = d // LANES
    part = tm // CONV_SPLIT
    span = CONV_PHASES * CONV_CHUNK
    first = (pl.program_id(0) % tiles_per_seq) == 0
    ubufs, cbufs = bufs[:CONV_SPLIT], bufs[CONV_SPLIT:]

    @pl.when(pl.program_id(0) == 0)
    def _():
        win_ref[...] = win_f32[...].astype(BF16)
        wout_ref[...] = wout_f32[...].astype(BF16)
        for k in range(CONV_WIDTH):
            dw16_ref[k * BF16_ROWS:(k + 1) * BF16_ROWS, :] = jnp.broadcast_to(
                dw_ref[k:k + 1, :], (BF16_ROWS, d)).astype(BF16)

    @pl.when(first)
    def _():
        ubufs[0][:, 0:CONV_HALO, :] = jnp.zeros((nslab, CONV_HALO, LANES), F32)

    @pl.when(jnp.logical_not(first))
    def _():
        ubufs[0][:, 0:CONV_HALO, :] = ubufs[-1][:, part:part + CONV_HALO, :]

    def glu_steps(gi):
        rows = slice(gi * part, (gi + 1) * part)
        h = _rms(x_ref[rows, :], ng_ref[...]).astype(BF16)

        def step(c0):
            def run():
                a_cols = slice(c0, c0 + MXU_COLS)
                g_cols = slice(d + c0, d + c0 + MXU_COLS)
                a = jnp.dot(h, win_ref[:, a_cols], preferred_element_type=F32) + bin_ref[:, a_cols]
                g = jnp.dot(h, win_ref[:, g_cols], preferred_element_type=F32) + bin_ref[:, g_cols]
                glu = a * _sigmoid(g)
                for i in range(MXU_COLS // LANES):
                    piece = glu[:, i * LANES:(i + 1) * LANES]
                    ubufs[gi][c0 // LANES + i, CONV_HALO:CONV_HALO + part, :] = piece
                    if gi + 1 < CONV_SPLIT:
                        ubufs[gi + 1][c0 // LANES + i, 0:CONV_HALO, :] = piece[part - CONV_HALO:, :]
            return run
        return [step(c0) for c0 in range(0, d, MXU_COLS)]

    def conv_steps(gi):
        def step(s):
            def run():
                lanes = slice(s * LANES, (s + 1) * LANES)
                def w16(k):
                    w = dw16_ref[k * BF16_ROWS:(k + 1) * BF16_ROWS, lanes]
                    return jnp.concatenate([w] * (CONV_CHUNK // BF16_ROWS), axis=0)

                for t0 in range(0, part, span):
                    acc = [jnp.zeros((CONV_CHUNK, LANES), F32) + dwb_ref[:, lanes]] * CONV_PHASES
                    run16 = [None] * CONV_PHASES
                    operand = {}
                    for k in range(CONV_WIDTH):
                        wk = w16(k)
                        for ph in range(CONV_PHASES):
                            c = ph + k - (CONV_WIDTH - 1)
                            if c not in operand:
                                operand[c] = ubufs[gi][s, pl.ds(CONV_HALO + t0 + c, CONV_CHUNK,
                                                                stride=CONV_PHASES), :].astype(BF16)
                            prod = operand[c] * wk
                            run16[ph] = prod if run16[ph] is None else run16[ph] + prod
                            if (k + 1) % CONV_BF16_TAPS == 0 or k == CONV_WIDTH - 1:
                                acc[ph] = acc[ph] + run16[ph].astype(F32)
                                run16[ph] = None
                        operand.pop(k - (CONV_WIDTH - 1), None)
                    for ph in range(CONV_PHASES):
                        cbufs[gi][s, pl.ds(t0 + ph, CONV_CHUNK, stride=CONV_PHASES), :] = acc[ph]
            return run
        return [step(s) for s in range(nslab)]

    def out_steps(gi):
        rows = slice(gi * part, (gi + 1) * part)
        c = jnp.concatenate([cbufs[gi][s] for s in range(nslab)], axis=1)
        mu = jnp.mean(c, axis=-1, keepdims=True)
        cc = c - mu
        y = cc * lax.rsqrt(jnp.mean(cc * cc, axis=-1, keepdims=True) + NORM_EPS)
        y = y * lng_ref[...] + lnb_ref[...]
        y = (y * _sigmoid(y)).astype(BF16)

        def step(c0):
            def run():
                cols = slice(c0, c0 + MXU_COLS)
                o_ref[rows, cols] = (x_ref[rows, cols] + bout_ref[:, cols]
                                     + jnp.dot(y, wout_ref[:, cols], preferred_element_type=F32))
            return run
        return [step(c0) for c0 in range(0, d, MXU_COLS)]

    _interleave(glu_steps(0), [])
    for gi in range(CONV_SPLIT):
        mxu_side = []
        if gi > 0:
            mxu_side += out_steps(gi - 1)
        if gi + 1 < CONV_SPLIT:
            mxu_side += glu_steps(gi + 1)
        _interleave(conv_steps(gi), mxu_side)
    _interleave(out_steps(CONV_SPLIT - 1), [])


def _conv_block(x, seq, layer, ng, w_in, b_in, dw, dw_b, ln_g, ln_b, w_out, b_out):
    t, d = x.shape
    row = lambda v: v.reshape(1, -1)
    body = functools.partial(_conv_body, tiles_per_seq=seq // CONV_TILE)
    part = CONV_TILE // CONV_SPLIT
    return pl.pallas_call(
        body,
        grid=(t // CONV_TILE,),
        in_specs=[
            pl.BlockSpec((CONV_TILE, d), lambda i: (i, 0)),
            _const_spec((1, d)),
            _layer_spec((d, 2 * d), layer),
            _const_spec((1, 2 * d)),
            _const_spec((CONV_WIDTH, d)),
            _const_spec((1, d)),
            _const_spec((1, d)),
            _const_spec((1, d)),
            _layer_spec((d, d), layer),
            _const_spec((1, d)),
        ],
        out_specs=pl.BlockSpec((CONV_TILE, d), lambda i: (i, 0)),
        out_shape=jax.ShapeDtypeStruct((t, d), F32),
        scratch_shapes=(
            [pltpu.VMEM((d, 2 * d), BF16), pltpu.VMEM((d, d), BF16),
             pltpu.VMEM((CONV_WIDTH * BF16_ROWS, d), BF16)]
            + [pltpu.VMEM((d // LANES, part + CONV_HALO, LANES), F32)] * CONV_SPLIT
            + [pltpu.VMEM((d // LANES, part, LANES), F32)] * CONV_SPLIT),
        compiler_params=_compiler_params(),
        name="conv_block",
    )(x, row(ng), w_in, row(b_in), dw, row(dw_b), row(ln_g), row(ln_b), w_out, row(b_out))


def _t5_causal_bucket(dist):
    n = np.maximum(dist, 0)
    max_exact = REL_BUCKETS // 2
    large = max_exact + (np.log(np.maximum(n, 1).astype(np.float32) / max_exact)
                         / math.log(REL_MAX_DIST / max_exact)
                         * (REL_BUCKETS - max_exact)).astype(np.int32)
    large = np.minimum(large, REL_BUCKETS - 1)
    return np.where(n < max_exact, n, large).astype(np.int32)


def _bias_body(rel_ref, bucket_ref, o_ref):
    g = pl.program_id(0)
    bucket = bucket_ref[...]
    acc = [jnp.full(bucket.shape, NEG_INF, F32)] * GROUP
    for b in range(REL_BUCKETS):
        hit = bucket == b
        acc = [jnp.where(hit, rel_ref[b, g * GROUP + hh] * LOG2E, acc[hh]) for hh in range(GROUP)]
    for hh in range(GROUP):
        o_ref[0, :, hh * WINDOW:(hh + 1) * WINDOW] = acc[hh]


def _band_bias(rel_bias):
    k_loc = np.arange(2 * WINDOW)[:, None]
    q_loc = np.arange(WINDOW)[None, :]
    dist = q_loc + WINDOW - k_loc
    bucket = np.where((dist >= 0) & (dist < WINDOW), _t5_causal_bucket(dist), -1)
    return pl.pallas_call(
        _bias_body,
        grid=(N_KV_HEADS,),
        in_specs=[
            pl.BlockSpec(memory_space=pltpu.SMEM),
            pl.BlockSpec((2 * WINDOW, WINDOW), lambda i: (0, 0)),
        ],
        out_specs=pl.BlockSpec((1, 2 * WINDOW, GROUP * WINDOW), lambda i: (i, 0, 0)),
        out_shape=jax.ShapeDtypeStruct((N_KV_HEADS, 2 * WINDOW, GROUP * WINDOW), F32),
        name="band_bias",
    )(rel_bias, jnp.asarray(bucket.astype(np.int32)))


def _attn_body(x_ref, ng_ref, wqkv_f32, bqkv_ref, kgain_ref, bias_ref, sinks_ref, wo_f32,
               bo_ref, o_ref, wqkv_ref, wo_ref, qbuf, kbuf, vbuf, abuf, pbuf, *sbufs,
               tiles_per_seq):
    tq, d = x_ref.shape
    kvd = N_KV_HEADS * HEAD_DIM
    nblk = tq // WINDOW
    first = (pl.program_id(0) % tiles_per_seq) == 0

    @pl.when(pl.program_id(0) == 0)
    def _():
        wqkv_ref[...] = wqkv_f32[...].T.astype(BF16)
        wo_ref[...] = wo_f32[...].T.astype(BF16)

    @pl.when(first)
    def _():
        kbuf[0:WINDOW, :] = jnp.zeros((WINDOW, kvd), BF16)
        vbuf[:, 0:WINDOW] = jnp.zeros((kvd, WINDOW), BF16)

    @pl.when(jnp.logical_not(first))
    def _():
        kbuf[0:WINDOW, :] = kbuf[tq:tq + WINDOW, :]
        vbuf[:, 0:WINDOW] = vbuf[:, tq:tq + WINDOW]

    h = _rms(x_ref[...], ng_ref[...]).astype(BF16)

    def proj_step(r0):
        def run():
            pbuf[r0:r0 + MXU_COLS, :] = lax.dot_general(
                wqkv_ref[r0:r0 + MXU_COLS, :], h, (((1,), (1,)), ((), ())),
                preferred_element_type=F32)
        return run

    def head_rows(r0):
        rows = slice(r0, r0 + HEAD_DIM)
        return pbuf[rows, :] + jnp.concatenate([bqkv_ref[rows, :]] * nblk, axis=1)

    def q_norm_step(hd):
        def run():
            qh = head_rows(hd * HEAD_DIM)
            ms = jnp.mean(qh * qh, axis=0, keepdims=True)
            qbuf[hd * HEAD_DIM:(hd + 1) * HEAD_DIM, :] = (
                qh * lax.rsqrt(ms + NORM_EPS)).astype(BF16)
        return run

    def kv_step():
        kn = []
        for g in range(N_KV_HEADS):
            kh = head_rows(d + g * HEAD_DIM)
            ms = jnp.mean(kh * kh, axis=0, keepdims=True)
            kn.append(kh * lax.rsqrt(ms + NORM_EPS))
            vbuf[g * HEAD_DIM:(g + 1) * HEAD_DIM, WINDOW:WINDOW + tq] = (
                head_rows(d + kvd + g * HEAD_DIM).astype(BF16))
        kt = jnp.concatenate(kn, axis=0)
        kbuf[WINDOW:WINDOW + tq, :] = (kt.T * kgain_ref[...]).astype(BF16)

    heads_per_chunk = MXU_COLS // HEAD_DIM
    proj = [proj_step(r0) for r0 in range(0, d + 2 * kvd, MXU_COLS)]
    proj[0]()
    for i in range(d // MXU_COLS):
        proj[i + 1]()
        for hd in range(i * heads_per_chunk, (i + 1) * heads_per_chunk):
            q_norm_step(hd)()
    kv_step()

    kk = lax.broadcasted_iota(jnp.int32, (2 * WINDOW, WINDOW), 0)
    first_key = jnp.where(first, WINDOW, 0)
    ones_rows = (lax.broadcasted_iota(jnp.int32, (BF16_ROWS, 2 * WINDOW), 0) == 0).astype(BF16)
    pairs = [(j, g) for j in range(nblk) for g in range(N_KV_HEADS)]

    def score_steps(n):
        j, g = pairs[n]
        kw = kbuf[j * WINDOW:(j + 2) * WINDOW, g * HEAD_DIM:(g + 1) * HEAD_DIM]

        def step(c0):
            def run():
                qt = jnp.concatenate(
                    [qbuf[(g * GROUP + c0 // WINDOW + i) * HEAD_DIM:
                          (g * GROUP + c0 // WINDOW + i + 1) * HEAD_DIM, j * WINDOW:(j + 1) * WINDOW]
                     for i in range(MXU_COLS // WINDOW)], axis=1)
                sbufs[n % 2][:, c0:c0 + MXU_COLS] = jnp.dot(kw, qt, preferred_element_type=F32)
            return run
        return [step(c0) for c0 in range(0, GROUP * WINDOW, MXU_COLS)]

    def softmax_steps(n):
        j, g = pairs[n]
        cols = slice(j * WINDOW, (j + 1) * WINDOW)
        band = slice(j * WINDOW, (j + 2) * WINDOW)
        va = jnp.concatenate([vbuf[g * HEAD_DIM:(g + 1) * HEAD_DIM, band], ones_rows], axis=0)

        def step(c0):
            def run():
                ps, ms = [], []
                for i in range(MXU_COLS // WINDOW):
                    blk = slice(c0 + i * WINDOW, c0 + (i + 1) * WINDOW)
                    s = sbufs[n % 2][:, blk] + bias_ref[g, :, blk]
                    if j == 0:
                        s = jnp.where(kk >= first_key, s, NEG_INF)
                    sink = sinks_ref[g * GROUP + c0 // WINDOW + i] * LOG2E
                    m = jnp.maximum(jnp.max(s, axis=0, keepdims=True), sink)
                    ps.append(jnp.exp2(s - m).astype(BF16))
                    ms.append((m, sink))
                ot = jnp.dot(va, jnp.concatenate(ps, axis=1), preferred_element_type=F32)
                for i, (m, sink) in enumerate(ms):
                    hd = g * GROUP + c0 // WINDOW + i
                    denom = ot[HEAD_DIM:HEAD_DIM + 1, i * WINDOW:(i + 1) * WINDOW] + jnp.exp2(sink - m)
                    abuf[hd * HEAD_DIM:(hd + 1) * HEAD_DIM, cols] = (
                        ot[0:HEAD_DIM, i * WINDOW:(i + 1) * WINDOW] * (1.0 / denom)).astype(BF16)
            return run
        return [step(c0) for c0 in range(0, GROUP * WINDOW, MXU_COLS)]

    _interleave(score_steps(0), [])
    for n in range(len(pairs)):
        _interleave(softmax_steps(n), score_steps(n + 1) if n + 1 < len(pairs) else [])

    def out_proj_step(r0):
        def run():
            pbuf[r0:r0 + MXU_COLS, :] = jnp.dot(wo_ref[r0:r0 + MXU_COLS, :], abuf[...],
                                                preferred_element_type=F32)
        return run

    def residual_step(r0):
        def run():
            cols = slice(r0, r0 + MXU_COLS)
            o_ref[:, cols] = x_ref[:, cols] + pbuf[cols, :].T + bo_ref[:, cols]
        return run

    out_proj_step(0)()
    for r0 in range(0, d, MXU_COLS):
        if r0 + MXU_COLS < d:
            out_proj_step(r0 + MXU_COLS)()
        residual_step(r0)()


def _attn_block(x, seq, layer, ng, w_qkv, b_qkv, q_g, k_g, sinks, w_o, b_o, bias):
    t, d = x.shape
    qkv_dim = w_qkv.shape[2]
    kvd = N_KV_HEADS * HEAD_DIM
    row = lambda v: v.reshape(1, -1)
    kgain = jnp.tile(q_g * k_g * (LOG2E / math.sqrt(HEAD_DIM)), N_KV_HEADS)
    body = functools.partial(_attn_body, tiles_per_seq=seq // TOKEN_TILE)
    return pl.pallas_call(
        body,
        grid=(t // TOKEN_TILE,),
        in_specs=[
            pl.BlockSpec((TOKEN_TILE, d), lambda i: (i, 0)),
            _const_spec((1, d)),
            _layer_spec((d, qkv_dim), layer),
            _const_spec((qkv_dim, LANES)),
            _const_spec((1, kvd)),
            _const_spec((N_KV_HEADS, 2 * WINDOW, GROUP * WINDOW)),
            pl.BlockSpec(memory_space=pltpu.SMEM),
            _layer_spec((d, d), layer),
            _const_spec((1, d)),
        ],
        out_specs=pl.BlockSpec((TOKEN_TILE, d), lambda i: (i, 0)),
        out_shape=jax.ShapeDtypeStruct((t, d), F32),
        scratch_shapes=[
            pltpu.VMEM((qkv_dim, d), BF16),
            pltpu.VMEM((d, d), BF16),
            pltpu.VMEM((d, TOKEN_TILE), BF16),
            pltpu.VMEM((TOKEN_TILE + WINDOW, kvd), BF16),
            pltpu.VMEM((kvd, TOKEN_TILE + WINDOW), BF16),
            pltpu.VMEM((d, TOKEN_TILE), BF16),
            pltpu.VMEM((qkv_dim, TOKEN_TILE), F32),
            pltpu.VMEM((2 * WINDOW, GROUP * WINDOW), F32),
            pltpu.VMEM((2 * WINDOW, GROUP * WINDOW), F32),
        ],
        compiler_params=_compiler_params(),
        name="attn_block",
    )(x, row(ng), w_qkv, jnp.broadcast_to(b_qkv[:, None], (qkv_dim, LANES)),
      row(kgain), bias, sinks, w_o, row(b_o))


def kernel(x, conv_norm_g, conv_w_in, conv_b_in, conv_dw, conv_dw_b, conv_ln_g, conv_ln_b, conv_w_out, conv_b_out, attn_norm_g, w_qkv, b_qkv, q_norm_g, k_norm_g, sinks, w_o, b_o, rel_bias, mlp_norm_g, w_up, w_down):
    b, s, d = x.shape
    assert s % TOKEN_TILE == 0 and TOKEN_TILE % WINDOW == 0 and d == N_HEADS * HEAD_DIM
    depth = mlp_norm_g.shape[0]
    bias = _band_bias(rel_bias)
    xf = x.reshape(b * s, d)
    for i in range(depth):
        j = i // 2
        if i % 2 == 0:
            xf = _conv_block(xf, s, j, conv_norm_g[j], conv_w_in, conv_b_in[j], conv_dw[j],
                             conv_dw_b[j], conv_ln_g[j], conv_ln_b[j], conv_w_out,
                             conv_b_out[j])
        else:
            xf = _attn_block(xf, s, j, attn_norm_g[j], w_qkv, b_qkv[j], q_norm_g[j],
                             k_norm_g[j], sinks[j], w_o, b_o[j], bias)
        xf = _mlp(xf, mlp_norm_g, w_up, w_down, i)
    return xf.reshape(b, s, d)
```

```python
import functools
import math

import numpy as np
import jax
import jax.numpy as jnp
from jax import lax
from jax.experimental import pallas as pl
from jax.experimental.pallas import tpu as pltpu

F32 = jnp.float32
BF16 = jnp.bfloat16

CONV_WIDTH = 31
N_HEADS = 16
N_KV_HEADS = 2
HEAD_DIM = 64
GROUP = N_HEADS // N_KV_HEADS
WINDOW = 128
REL_BUCKETS = 32
REL_MAX_DIST = 128
NORM_EPS = 1e-6
NEG_INF = -1e30
LOG2E = math.log2(math.e)

LANES = 128
BF16_ROWS = 16
MXU_COLS = 256
VMEM_LIMIT_BYTES = 58 * 1024 * 1024

TOKEN_TILE = 1024
FF_CHUNK = 512
CONV_HALO = 32
CONV_PHASES = 4
CONV_CHUNK = 32
CONV_BF16_TAPS = 8
CONV_TILE = 1024
CONV_SPLIT = 2


def _rms(x, g):
    return x * lax.rsqrt(jnp.mean(x * x, axis=-1, keepdims=True) + NORM_EPS) * g


def _sigmoid(x):
    return 1.0 / (1.0 + jnp.exp2(x * -LOG2E))


def _interleave(main, side):
    n, m, j = len(main), len(side), 0
    for i, step in enumerate(main):
        step()
        while j < m and (j + 1) * n <= (i + 1) * m:
            side[j]()
            j += 1
    for step in side[j:]:
        step()


def _const_spec(shape):
    n = len(shape)
    return pl.BlockSpec(shape, lambda i: (0,) * n, pipeline_mode=pl.Buffered(1))


def _layer_spec(shape, layer):
    n = len(shape)
    return pl.BlockSpec((None,) + tuple(shape), lambda i: (layer,) + (0,) * n,
                        pipeline_mode=pl.Buffered(1))


def _compiler_params():
    return pltpu.CompilerParams(dimension_semantics=("arbitrary",),
                                vmem_limit_bytes=VMEM_LIMIT_BYTES)


def _mlp_body(x_ref, g_ref, wu_ref, wd_ref, o_ref):
    x = x_ref[...]
    h = _rms(x, g_ref[...]).astype(BF16)
    acc = x
    for c in range(wu_ref.shape[1] // FF_CHUNK):
        cols = slice(c * FF_CHUNK, (c + 1) * FF_CHUNK)
        hid = jnp.dot(h, wu_ref[:, cols].astype(BF16), preferred_element_type=F32)
        hid = jnp.square(jnp.maximum(hid, 0.0)).astype(BF16)
        acc = acc + jnp.dot(hid, wd_ref[cols, :].astype(BF16), preferred_element_type=F32)
    o_ref[...] = acc


def _mlp(x, g, w_up, w_down, layer):
    t, d = x.shape
    f = w_up.shape[2]
    return pl.pallas_call(
        _mlp_body,
        grid=(t // TOKEN_TILE,),
        in_specs=[
            pl.BlockSpec((TOKEN_TILE, d), lambda i: (i, 0)),
            _layer_spec((1, d), layer),
            _layer_spec((d, f), layer),
            _layer_spec((f, d), layer),
        ],
        out_specs=pl.BlockSpec((TOKEN_TILE, d), lambda i: (i, 0)),
        out_shape=jax.ShapeDtypeStruct((t, d), F32),
        compiler_params=_compiler_params(),
        name="mlp_block",
    )(x, g.reshape(g.shape[0], 1, d), w_up, w_down)


def _conv_body(x_ref, ng_ref, win_f32, bin_ref, dw_ref, dwb_ref, lng_ref, lnb_ref,
               wout_f32, bout_ref, o_ref, win_ref, wout_ref, dw16_ref, *bufs, tiles_per_seq):
    tm, d = x_ref.shape
    nslab = d // LANES
    part = tm // CONV_SPLIT
    span = CONV_PHASES * CONV_CHUNK
    first = (pl.program_id(0) % tiles_per_seq) == 0
    ubufs, cbufs = bufs[:CONV_SPLIT], bufs[CONV_SPLIT:]

    @pl.when(pl.program_id(0) == 0)
    def _():
        win_ref[...] = win_f32[...].astype(BF16)
        wout_ref[...] = wout_f32[...].astype(BF16)
        for k in range(CONV_WIDTH):
            dw16_ref[k * BF16_ROWS:(k + 1) * BF16_ROWS, :] = jnp.broadcast_to(
                dw_ref[k:k + 1, :], (BF16_ROWS, d)).astype(BF16)

    @pl.when(first)
    def _():
        ubufs[0][:, 0:CONV_HALO, :] = jnp.zeros((nslab, CONV_HALO, LANES), F32)

    @pl.when(jnp.logical_not(first))
    def _():
        ubufs[0][:, 0:CONV_HALO, :] = ubufs[-1][:, part:part + CONV_HALO, :]

    def glu_steps(gi):
        rows = slice(gi * part, (gi + 1) * part)
        h = _rms(x_ref[rows, :], ng_ref[...]).astype(BF16)

        def step(c0):
            def run():
                a_cols = slice(c0, c0 + MXU_COLS)
                g_cols = slice(d + c0, d + c0 + MXU_COLS)
                a = jnp.dot(h, win_ref[:, a_cols], preferred_element_type=F32) + bin_ref[:, a_cols]
                g = jnp.dot(h, win_ref[:, g_cols], preferred_element_type=F32) + bin_ref[:, g_cols]
                glu = a * _sigmoid(g)
                for i in range(MXU_COLS // LANES):
                    piece = glu[:, i * LANES:(i + 1) * LANES]
                    ubufs[gi][c0 // LANES + i, CONV_HALO:CONV_HALO + part, :] = piece
                    if gi + 1 < CONV_SPLIT:
                        ubufs[gi + 1][c0 // LANES + i, 0:CONV_HALO, :] = piece[part - CONV_HALO:, :]
            return run
        return [step(c0) for c0 in range(0, d, MXU_COLS)]

    def conv_steps(gi):
        def step(s):
            def run():
                lanes = slice(s * LANES, (s + 1) * LANES)
                def w16(k):
                    w = dw16_ref[k * BF16_ROWS:(k + 1) * BF16_ROWS, lanes]
                    return jnp.concatenate([w] * (CONV_CHUNK // BF16_ROWS), axis=0)

                for t0 in range(0, part, span):
                    acc = [jnp.zeros((CONV_CHUNK, LANES), F32) + dwb_ref[:, lanes]] * CONV_PHASES
                    run16 = [None] * CONV_PHASES
                    operand = {}
                    for k in range(CONV_WIDTH):
                        wk = w16(k)
                        for ph in range(CONV_PHASES):
                            c = ph + k - (CONV_WIDTH - 1)
                            if c not in operand:
                                operand[c] = ubufs[gi][s, pl.ds(CONV_HALO + t0 + c, CONV_CHUNK,
                                                                stride=CONV_PHASES), :].astype(BF16)
                            prod = operand[c] * wk
                            run16[ph] = prod if run16[ph] is None else run16[ph] + prod
                            if (k + 1) % CONV_BF16_TAPS == 0 or k == CONV_WIDTH - 1:
                                acc[ph] = acc[ph] + run16[ph].astype(F32)
                                run16[ph] = None
                        operand.pop(k - (CONV_WIDTH - 1), None)
                    for ph in range(CONV_PHASES):
                        cbufs[gi][s, pl.ds(t0 + ph, CONV_CHUNK, stride=CONV_PHASES), :] = acc[ph]
            return run
        return [step(s) for s in range(nslab)]

    def out_steps(gi):
        rows = slice(gi * part, (gi + 1) * part)
        c = jnp.concatenate([cbufs[gi][s] for s in range(nslab)], axis=1)
        mu = jnp.mean(c, axis=-1, keepdims=True)
        cc = c - mu
        y = cc * lax.rsqrt(jnp.mean(cc * cc, axis=-1, keepdims=True) + NORM_EPS)
        y = y * lng_ref[...] + lnb_ref[...]
        y = (y * _sigmoid(y)).astype(BF16)

        def step(c0):
            def run():
                cols = slice(c0, c0 + MXU_COLS)
                o_ref[rows, cols] = (x_ref[rows, cols] + bout_ref[:, cols]
                                     + jnp.dot(y, wout_ref[:, cols], preferred_element_type=F32))
            return run
        return [step(c0) for c0 in range(0, d, MXU_COLS)]

    _interleave(glu_steps(0), [])
    for gi in range(CONV_SPLIT):
        mxu_side = []
        if gi > 0:
            mxu_side += out_steps(gi - 1)
        if gi + 1 < CONV_SPLIT:
            mxu_side += glu_steps(gi + 1)
        _interleave(conv_steps(gi), mxu_side)
    _interleave(out_steps(CONV_SPLIT - 1), [])


def _conv_block(x, seq, layer, ng, w_in, b_in, dw, dw_b, ln_g, ln_b, w_out, b_out):
    t, d = x.shape
    row = lambda v: v.reshape(1, -1)
    body = functools.partial(_conv_body, tiles_per_seq=seq // CONV_TILE)
    part = CONV_TILE // CONV_SPLIT
    return pl.pallas_call(
        body,
        grid=(t // CONV_TILE,),
        in_specs=[
            pl.BlockSpec((CONV_TILE, d), lambda i: (i, 0)),
            _const_spec((1, d)),
            _layer_spec((d, 2 * d), layer),
            _const_spec((1, 2 * d)),
            _const_spec((CONV_WIDTH, d)),
            _const_spec((1, d)),
            _const_spec((1, d)),
            _const_spec((1, d)),
            _layer_spec((d, d), layer),
            _const_spec((1, d)),
        ],
        out_specs=pl.BlockSpec((CONV_TILE, d), lambda i: (i, 0)),
        out_shape=jax.ShapeDtypeStruct((t, d), F32),
        scratch_shapes=(
            [pltpu.VMEM((d, 2 * d), BF16), pltpu.VMEM((d, d), BF16),
             pltpu.VMEM((CONV_WIDTH * BF16_ROWS, d), BF16)]
            + [pltpu.VMEM((d // LANES, part + CONV_HALO, LANES), F32)] * CONV_SPLIT
            + [pltpu.VMEM((d // LANES, part, LANES), F32)] * CONV_SPLIT),
        compiler_params=_compiler_params(),
        name="conv_block",
    )(x, row(ng), w_in, row(b_in), dw, row(dw_b), row(ln_g), row(ln_b), w_out, row(b_out))


def _t5_causal_bucket(dist):
    n = np.maximum(dist, 0)
    max_exact = REL_BUCKETS // 2
    large = max_exact + (np.log(np.maximum(n, 1).astype(np.float32) / max_exact)
                         / math.log(REL_MAX_DIST / max_exact)
                         * (REL_BUCKETS - max_exact)).astype(np.int32)
    large = np.minimum(large, REL_BUCKETS - 1)
    return np.where(n < max_exact, n, large).astype(np.int32)


def _bias_body(rel_ref, bucket_ref, o_ref):
    g = pl.program_id(0)
    bucket = bucket_ref[...]
    acc = [jnp.full(bucket.shape, NEG_INF, F32)] * GROUP
    for b in range(REL_BUCKETS):
        hit = bucket == b
        acc = [jnp.where(hit, rel_ref[b, g * GROUP + hh] * LOG2E, acc[hh]) for hh in range(GROUP)]
    for hh in range(GROUP):
        o_ref[0, :, hh * WINDOW:(hh + 1) * WINDOW] = acc[hh]


def _band_bias(rel_bias):
    k_loc = np.arange(2 * WINDOW)[:, None]
    q_loc = np.arange(WINDOW)[None, :]
    dist = q_loc + WINDOW - k_loc
    bucket = np.where((dist >= 0) & (dist < WINDOW), _t5_causal_bucket(dist), -1)
    return pl.pallas_call(
        _bias_body,
        grid=(N_KV_HEADS,),
        in_specs=[
            pl.BlockSpec(memory_space=pltpu.SMEM),
            pl.BlockSpec((2 * WINDOW, WINDOW), lambda i: (0, 0)),
        ],
        out_specs=pl.BlockSpec((1, 2 * WINDOW, GROUP * WINDOW), lambda i: (i, 0, 0)),
        out_shape=jax.ShapeDtypeStruct((N_KV_HEADS, 2 * WINDOW, GROUP * WINDOW), F32),
        name="band_bias",
    )(rel_bias, jnp.asarray(bucket.astype(np.int32)))


def _attn_body(x_ref, ng_ref, wqkv_f32, bqkv_ref, kgain_ref, bias_ref, sinks_ref, wo_f32,
               bo_ref, o_ref, wqkv_ref, wo_ref, qbuf, kbuf, vbuf, abuf, pbuf, *sbufs,
               tiles_per_seq):
    tq, d = x_ref.shape
    kvd = N_KV_HEADS * HEAD_DIM
    nblk = tq // WINDOW
    first = (pl.program_id(0) % tiles_per_seq) == 0

    @pl.when(pl.program_id(0) == 0)
    def _():
        wqkv_ref[...] = wqkv_f32[...].T.astype(BF16)
        wo_ref[...] = wo_f32[...].T.astype(BF16)

    @pl.when(first)
    def _():
        kbuf[0:WINDOW, :] = jnp.zeros((WINDOW, kvd), BF16)
        vbuf[:, 0:WINDOW] = jnp.zeros((kvd, WINDOW), BF16)

    @pl.when(jnp.logical_not(first))
    def _():
        kbuf[0:WINDOW, :] = kbuf[tq:tq + WINDOW, :]
        vbuf[:, 0:WINDOW] = vbuf[:, tq:tq + WINDOW]

    h = _rms(x_ref[...], ng_ref[...]).astype(BF16)

    def proj_step(r0):
        def run():
            pbuf[r0:r0 + MXU_COLS, :] = lax.dot_general(
                wqkv_ref[r0:r0 + MXU_COLS, :], h, (((1,), (1,)), ((), ())),
                preferred_element_type=F32)
        return run

    def head_rows(r0):
        rows = slice(r0, r0 + HEAD_DIM)
        return pbuf[rows, :] + jnp.concatenate([bqkv_ref[rows, :]] * nblk, axis=1)

    def q_norm_step(hd):
        def run():
            qh = head_rows(hd * HEAD_DIM)
            ms = jnp.mean(qh * qh, axis=0, keepdims=True)
            qbuf[hd * HEAD_DIM:(hd + 1) * HEAD_DIM, :] = (
                qh * lax.rsqrt(ms + NORM_EPS)).astype(BF16)
        return run

    def kv_step():
        kn = []
        for g in range(N_KV_HEADS):
            kh = head_rows(d + g * HEAD_DIM)
            ms = jnp.mean(kh * kh, axis=0, keepdims=True)
            kn.append(kh * lax.rsqrt(ms + NORM_EPS))
            vbuf[g * HEAD_DIM:(g + 1) * HEAD_DIM, WINDOW:WINDOW + tq] = (
                head_rows(d + kvd + g * HEAD_DIM).astype(BF16))
        kt = jnp.concatenate(kn, axis=0)
        kbuf[WINDOW:WINDOW + tq, :] = (kt.T * kgain_ref[...]).astype(BF16)

    heads_per_chunk = MXU_COLS // HEAD_DIM
    proj = [proj_step(r0) for r0 in range(0, d + 2 * kvd, MXU_COLS)]
    proj[0]()
    for i in range(d // MXU_COLS):
        proj[i + 1]()
        for hd in range(i * heads_per_chunk, (i + 1) * heads_per_chunk):
            q_norm_step(hd)()
    kv_step()

    kk = lax.broadcasted_iota(jnp.int32, (2 * WINDOW, WINDOW), 0)
    first_key = jnp.where(first, WINDOW, 0)
    ones_rows = (lax.broadcasted_iota(jnp.int32, (BF16_ROWS, 2 * WINDOW), 0) == 0).astype(BF16)
    pairs = [(j, g) for j in range(nblk) for g in range(N_KV_HEADS)]

    def score_steps(n):
        j, g = pairs[n]
        kw = kbuf[j * WINDOW:(j + 2) * WINDOW, g * HEAD_DIM:(g + 1) * HEAD_DIM]

        def step(c0):
            def run():
                qt = jnp.concatenate(
                    [qbuf[(g * GROUP + c0 // WINDOW + i) * HEAD_DIM:
                          (g * GROUP + c0 // WINDOW + i + 1) * HEAD_DIM, j * WINDOW:(j + 1) * WINDOW]
                     for i in range(MXU_COLS // WINDOW)], axis=1)
                sbufs[n % 2][:, c0:c0 + MXU_COLS] = jnp.dot(kw, qt, preferred_element_type=F32)
            return run
        return [step(c0) for c0 in range(0, GROUP * WINDOW, MXU_COLS)]

    def softmax_steps(n):
        j, g = pairs[n]
        cols = slice(j * WINDOW, (j + 1) * WINDOW)
        band = slice(j * WINDOW, (j + 2) * WINDOW)
        va = jnp.concatenate([vbuf[g * HEAD_DIM:(g + 1) * HEAD_DIM, band], ones_rows], axis=0)

        def step(c0):
            def run():
                ps, ms = [], []
                for i in range(MXU_COLS // WINDOW):
                    blk = slice(c0 + i * WINDOW, c0 + (i + 1) * WINDOW)
                    s = sbufs[n % 2][:, blk] + bias_ref[g, :, blk]
                    if j == 0:
                        s = jnp.where(kk >= first_key, s, NEG_INF)
                    sink = sinks_ref[g * GROUP + c0 // WINDOW + i] * LOG2E
                    m = jnp.maximum(jnp.max(s, axis=0, keepdims=True), sink)
                    ps.append(jnp.exp2(s - m).astype(BF16))
                    ms.append((m, sink))
                ot = jnp.dot(va, jnp.concatenate(ps, axis=1), preferred_element_type=F32)
                for i, (m, sink) in enumerate(ms):
                    hd = g * GROUP + c0 // WINDOW + i
                    denom = ot[HEAD_DIM:HEAD_DIM + 1, i * WINDOW:(i + 1) * WINDOW] + jnp.exp2(sink - m)
                    abuf[hd * HEAD_DIM:(hd + 1) * HEAD_DIM, cols] = (
                        ot[0:HEAD_DIM, i * WINDOW:(i + 1) * WINDOW] * (1.0 / denom)).astype(BF16)
            return run
        return [step(c0) for c0 in range(0, GROUP * WINDOW, MXU_COLS)]

    _interleave(score_steps(0), [])
    for n in range(len(pairs)):
        _interleave(softmax_steps(n), score_steps(n + 1) if n + 1 < len(pairs) else [])

    def out_proj_step(r0):
        def run():
            pbuf[r0:r0 + MXU_COLS, :] = jnp.dot(wo_ref[r0:r0 + MXU_COLS, :], abuf[...],
                                                preferred_element_type=F32)
        return run

    def residual_step(r0):
        def run():
            cols = slice(r0, r0 + MXU_COLS)
            o_ref[:, cols] = x_ref[:, cols] + pbuf[cols, :].T + bo_ref[:, cols]
        return run

    out_proj_step(0)()
    for r0 in range(0, d, MXU_COLS):
        if r0 + MXU_COLS < d:
            out_proj_step(r0 + MXU_COLS)()
        residual_step(r0)()


def _attn_block(x, seq, layer, ng, w_qkv, b_qkv, q_g, k_g, sinks, w_o, b_o, bias):
    t, d = x.shape
    qkv_dim = w_qkv.shape[2]
    kvd = N_KV_HEADS * HEAD_DIM
    row = lambda v: v.reshape(1, -1)
    kgain = jnp.tile(q_g * k_g * (LOG2E / math.sqrt(HEAD_DIM)), N_KV_HEADS)
    body = functools.partial(_attn_body, tiles_per_seq=seq // TOKEN_TILE)
    return pl.pallas_call(
        body,
        grid=(t // TOKEN_TILE,),
        in_specs=[
            pl.BlockSpec((TOKEN_TILE, d), lambda i: (i, 0)),
            _const_spec((1, d)),
            _layer_spec((d, qkv_dim), layer),
            _const_spec((qkv_dim, LANES)),
            _const_spec((1, kvd)),
            _const_spec((N_KV_HEADS, 2 * WINDOW, GROUP * WINDOW)),
            pl.BlockSpec(memory_space=pltpu.SMEM),
            _layer_spec((d, d), layer),
            _const_spec((1, d)),
        ],
        out_specs=pl.BlockSpec((TOKEN_TILE, d), lambda i: (i, 0)),
        out_shape=jax.ShapeDtypeStruct((t, d), F32),
        scratch_shapes=[
            pltpu.VMEM((qkv_dim, d), BF16),
            pltpu.VMEM((d, d), BF16),
            pltpu.VMEM((d, TOKEN_TILE), BF16),
            pltpu.VMEM((TOKEN_TILE + WINDOW, kvd), BF16),
            pltpu.VMEM((kvd, TOKEN_TILE + WINDOW), BF16),
            pltpu.VMEM((d, TOKEN_TILE), BF16),
            pltpu.VMEM((qkv_dim, TOKEN_TILE), F32),
            pltpu.VMEM((2 * WINDOW, GROUP * WINDOW), F32),
            pltpu.VMEM((2 * WINDOW, GROUP * WINDOW), F32),
        ],
        compiler_params=_compiler_params(),
        name="attn_block",
    )(x, row(ng), w_qkv, jnp.broadcast_to(b_qkv[:, None], (qkv_dim, LANES)),
      row(kgain), bias, sinks, w_o, row(b_o))


def kernel(x, conv_norm_g, conv_w_in, conv_b_in, conv_dw, conv_dw_b, conv_ln_g, conv_ln_b, conv_w_out, conv_b_out, attn_norm_g, w_qkv, b_qkv, q_norm_g, k_norm_g, sinks, w_o, b_o, rel_bias, mlp_norm_g, w_up, w_down):
    b, s, d = x.shape
    assert s % TOKEN_TILE == 0 and TOKEN_TILE % WINDOW == 0 and d == N_HEADS * HEAD_DIM
    depth = mlp_norm_g.shape[0]
    bias = _band_bias(rel_bias)
    xf = x.reshape(b * s, d)
    for i in range(depth):
        j = i // 2
        if i % 2 == 0:
            xf = _conv_block(xf, s, j, conv_norm_g[j], conv_w_in, conv_b_in[j], conv_dw[j],
                             conv_dw_b[j], conv_ln_g[j], conv_ln_b[j], conv_w_out,
                             conv_b_out[j])
        else:
            xf = _attn_block(xf, s, j, attn_norm_g[j], w_qkv, b_qkv[j], q_norm_g[j],
                             k_norm_g[j], sinks[j], w_o, b_o[j], bias)
        xf = _mlp(xf, mlp_norm_g, w_up, w_down, i)
    return xf.reshape(b, s, d)
```

```python
import functools
import math

import numpy as np
import jax
import jax.numpy as jnp
from jax import lax
from jax.experimental import pallas as pl
from jax.experimental.pallas import tpu as pltpu

F32 = jnp.float32
BF16 = jnp.bfloat16

CONV_WIDTH = 31
N_HEADS = 16
N_KV_HEADS = 2
HEAD_DIM = 64
GROUP = N_HEADS // N_KV_HEADS
WINDOW = 128
REL_BUCKETS = 32
REL_MAX_DIST = 128
NORM_EPS = 1e-6
NEG_INF = -1e30
LOG2E = math.log2(math.e)

LANES = 128
BF16_ROWS = 16
MXU_COLS = 256
VMEM_LIMIT_BYTES = 58 * 1024 * 1024

TOKEN_TILE = 1024
FF_CHUNK = 512
CONV_HALO = 32
CONV_PHASES = 4
CONV_CHUNK = 32
CONV_BF16_TAPS = 8
CONV_TILE = 1024
CONV_SPLIT = 1


def _rms(x, g):
    return x * lax.rsqrt(jnp.mean(x * x, axis=-1, keepdims=True) + NORM_EPS) * g


def _sigmoid(x):
    return 1.0 / (1.0 + jnp.exp2(x * -LOG2E))


def _interleave(main, side):
    n, m, j = len(main), len(side), 0
    for i, step in enumerate(main):
        step()
        while j < m and (j + 1) * n <= (i + 1) * m:
            side[j]()
            j += 1
    for step in side[j:]:
        step()


def _const_spec(shape):
    n = len(shape)
    return pl.BlockSpec(shape, lambda i: (0,) * n, pipeline_mode=pl.Buffered(1))


def _layer_spec(shape, layer):
    n = len(shape)
    return pl.BlockSpec((None,) + tuple(shape), lambda i: (layer,) + (0,) * n,
                        pipeline_mode=pl.Buffered(1))


def _compiler_params():
    return pltpu.CompilerParams(dimension_semantics=("arbitrary",),
                                vmem_limit_bytes=VMEM_LIMIT_BYTES)


def _mlp_body(x_ref, g_ref, wu_ref, wd_ref, o_ref):
    x = x_ref[...]
    h = _rms(x, g_ref[...]).astype(BF16)
    acc = x
    for c in range(wu_ref.shape[1] // FF_CHUNK):
        cols = slice(c * FF_CHUNK, (c + 1) * FF_CHUNK)
        hid = jnp.dot(h, wu_ref[:, cols].astype(BF16), preferred_element_type=F32)
        hid = jnp.square(jnp.maximum(hid, 0.0)).astype(BF16)
        acc = acc + jnp.dot(hid, wd_ref[cols, :].astype(BF16), preferred_element_type=F32)
    o_ref[...] = acc


def _mlp(x, g, w_up, w_down, layer):
    t, d = x.shape
    f = w_up.shape[2]
    return pl.pallas_call(
        _mlp_body,
        grid=(t // TOKEN_TILE,),
        in_specs=[
            pl.BlockSpec((TOKEN_TILE, d), lambda i: (i, 0)),
            _layer_spec((1, d), layer),
            _layer_spec((d, f), layer),
            _layer_spec((f, d), layer),
        ],
        out_specs=pl.BlockSpec((TOKEN_TILE, d), lambda i: (i, 0)),
        out_shape=jax.ShapeDtypeStruct((t, d), F32),
        compiler_params=_compiler_params(),
        name="mlp_block",
    )(x, g.reshape(g.shape[0], 1, d), w_up, w_down)


def _conv_body(x_ref, ng_ref, win_f32, bin_ref, dw_ref, dwb_ref, lng_ref, lnb_ref,
               wout_f32, bout_ref, o_ref, win_ref, wout_ref, dw16_ref, *bufs, tiles_per_seq):
    tm, d = x_ref.shape
    nslab = d // LANES
    part = tm // CONV_SPLIT
    span = CONV_PHASES * CONV_CHUNK
    first = (pl.program_id(0) % tiles_per_seq) == 0
    ubufs, cbufs = bufs[:CONV_SPLIT], bufs[CONV_SPLIT:]

    @pl.when(pl.program_id(0) == 0)
    def _():
        win_ref[...] = win_f32[...].astype(BF16)
        wout_ref[...] = wout_f32[...].astype(BF16)
        for k in range(CONV_WIDTH):
            dw16_ref[k * BF16_ROWS:(k + 1) * BF16_ROWS, :] = jnp.broadcast_to(
                dw_ref[k:k + 1, :], (BF16_ROWS, d)).astype(BF16)

    @pl.when(first)
    def _():
        ubufs[0][:, 0:CONV_HALO, :] = jnp.zeros((nslab, CONV_HALO, LANES), F32)

    @pl.when(jnp.logical_not(first))
    def _():
        ubufs[0][:, 0:CONV_HALO, :] = ubufs[-1][:, part:part + CONV_HALO, :]

    def glu_steps(gi):
        rows = slice(gi * part, (gi + 1) * part)
        h = _rms(x_ref[rows, :], ng_ref[...]).astype(BF16)

        def step(c0):
            def run():
                a_cols = slice(c0, c0 + MXU_COLS)
                g_cols = slice(d + c0, d + c0 + MXU_COLS)
                a = jnp.dot(h, win_ref[:, a_cols], preferred_element_type=F32) + bin_ref[:, a_cols]
                g = jnp.dot(h, win_ref[:, g_cols], preferred_element_type=F32) + bin_ref[:, g_cols]
                glu = a * _sigmoid(g)
                for i in range(MXU_COLS // LANES):
                    piece = glu[:, i * LANES:(i + 1) * LANES]
                    ubufs[gi][c0 // LANES + i, CONV_HALO:CONV_HALO + part, :] = piece
                    if gi + 1 < CONV_SPLIT:
                        ubufs[gi + 1][c0 // LANES + i, 0:CONV_HALO, :] = piece[part - CONV_HALO:, :]
            return run
        return [step(c0) for c0 in range(0, d, MXU_COLS)]

    def conv_steps(gi):
        def step(s):
            def run():
                lanes = slice(s * LANES, (s + 1) * LANES)
                def w16(k):
                    w = dw16_ref[k * BF16_ROWS:(k + 1) * BF16_ROWS, lanes]
                    return jnp.concatenate([w] * (CONV_CHUNK // BF16_ROWS), axis=0)

                for t0 in range(0, part, span):
                    acc = [jnp.zeros((CONV_CHUNK, LANES), F32) + dwb_ref[:, lanes]] * CONV_PHASES
                    run16 = [None] * CONV_PHASES
                    operand = {}
                    for k in range(CONV_WIDTH):
                        wk = w16(k)
                        for ph in range(CONV_PHASES):
                            c = ph + k - (CONV_WIDTH - 1)
                            if c not in operand:
                                operand[c] = ubufs[gi][s, pl.ds(CONV_HALO + t0 + c, CONV_CHUNK,
                                                                stride=CONV_PHASES), :].astype(BF16)
                            prod = operand[c] * wk
                            run16[ph] = prod if run16[ph] is None else run16[ph] + prod
                            if (k + 1) % CONV_BF16_TAPS == 0 or k == CONV_WIDTH - 1:
                                acc[ph] = acc[ph] + run16[ph].astype(F32)
                                run16[ph] = None
                        operand.pop(k - (CONV_WIDTH - 1), None)
                    for ph in range(CONV_PHASES):
                        cbufs[gi][s, pl.ds(t0 + ph, CONV_CHUNK, stride=CONV_PHASES), :] = acc[ph]
            return run
        return [step(s) for s in range(nslab)]

    def out_steps(gi):
        rows = slice(gi * part, (gi + 1) * part)
        c = jnp.concatenate([cbufs[gi][s] for s in range(nslab)], axis=1)
        mu = jnp.mean(c, axis=-1, keepdims=True)
        cc = c - mu
        y = cc * lax.rsqrt(jnp.mean(cc * cc, axis=-1, keepdims=True) + NORM_EPS)
        y = y * lng_ref[...] + lnb_ref[...]
        y = (y * _sigmoid(y)).astype(BF16)

        def step(c0):
            def run():
                cols = slice(c0, c0 + MXU_COLS)
                o_ref[rows, cols] = (x_ref[rows, cols] + bout_ref[:, cols]
                                     + jnp.dot(y, wout_ref[:, cols], preferred_element_type=F32))
            return run
        return [step(c0) for c0 in range(0, d, MXU_COLS)]

    _interleave(glu_steps(0), [])
    for gi in range(CONV_SPLIT):
        mxu_side = []
        if gi > 0:
            mxu_side += out_steps(gi - 1)
        if gi + 1 < CONV_SPLIT:
            mxu_side += glu_steps(gi + 1)
        _interleave(conv_steps(gi), mxu_side)
    _interleave(out_steps(CONV_SPLIT - 1), [])


def _conv_block(x, seq, layer, ng, w_in, b_in, dw, dw_b, ln_g, ln_b, w_out, b_out):
    t, d = x.shape
    row = lambda v: v.reshape(1, -1)
    body = functools.partial(_conv_body, tiles_per_seq=seq // CONV_TILE)
    part = CONV_TILE // CONV_SPLIT
    return pl.pallas_call(
        body,
        grid=(t // CONV_TILE,),
        in_specs=[
            pl.BlockSpec((CONV_TILE, d), lambda i: (i, 0)),
            _const_spec((1, d)),
            _layer_spec((d, 2 * d), layer),
            _const_spec((1, 2 * d)),
            _const_spec((CONV_WIDTH, d)),
            _const_spec((1, d)),
            _const_spec((1, d)),
            _const_spec((1, d)),
            _layer_spec((d, d), layer),
            _const_spec((1, d)),
        ],
        out_specs=pl.BlockSpec((CONV_TILE, d), lambda i: (i, 0)),
        out_shape=jax.ShapeDtypeStruct((t, d), F32),
        scratch_shapes=(
            [pltpu.VMEM((d, 2 * d), BF16), pltpu.VMEM((d, d), BF16),
             pltpu.VMEM((CONV_WIDTH * BF16_ROWS, d), BF16)]
            + [pltpu.VMEM((d // LANES, part + CONV_HALO, LANES), F32)] * CONV_SPLIT
            + [pltpu.VMEM((d // LANES, part, LANES), F32)] * CONV_SPLIT),
        compiler_params=_compiler_params(),
        name="conv_block",
    )(x, row(ng), w_in, row(b_in), dw, row(dw_b), row(ln_g), row(ln_b), w_out, row(b_out))


def _t5_causal_bucket(dist):
    n = np.maximum(dist, 0)
    max_exact = REL_BUCKETS // 2
    large = max_exact + (np.log(np.maximum(n, 1).astype(np.float32) / max_exact)
                         / math.log(REL_MAX_DIST / max_exact)
                         * (REL_BUCKETS - max_exact)).astype(np.int32)
    large = np.minimum(large, REL_BUCKETS - 1)
    return np.where(n < max_exact, n, large).astype(np.int32)


def _bias_body(rel_ref, bucket_ref, o_ref):
    g = pl.program_id(0)
    bucket = bucket_ref[...]
    acc = [jnp.full(bucket.shape, NEG_INF, F32)] * GROUP
    for b in range(REL_BUCKETS):
        hit = bucket == b
        acc = [jnp.where(hit, rel_ref[b, g * GROUP + hh] * LOG2E, acc[hh]) for hh in range(GROUP)]
    for hh in range(GROUP):
        o_ref[0, :, hh * WINDOW:(hh + 1) * WINDOW] = acc[hh]


def _band_bias(rel_bias):
    k_loc = np.arange(2 * WINDOW)[:, None]
    q_loc = np.arange(WINDOW)[None, :]
    dist = q_loc + WINDOW - k_loc
    bucket = np.where((dist >= 0) & (dist < WINDOW), _t5_causal_bucket(dist), -1)
    return pl.pallas_call(
        _bias_body,
        grid=(N_KV_HEADS,),
        in_specs=[
            pl.BlockSpec(memory_space=pltpu.SMEM),
            pl.BlockSpec((2 * WINDOW, WINDOW), lambda i: (0, 0)),
        ],
        out_specs=pl.BlockSpec((1, 2 * WINDOW, GROUP * WINDOW), lambda i: (i, 0, 0)),
        out_shape=jax.ShapeDtypeStruct((N_KV_HEADS, 2 * WINDOW, GROUP * WINDOW), F32),
        name="band_bias",
    )(rel_bias, jnp.asarray(bucket.astype(np.int32)))


def _attn_body(x_ref, ng_ref, wqkv_f32, bqkv_ref, kgain_ref, bias_ref, sinks_ref, wo_f32,
               bo_ref, o_ref, wqkv_ref, wo_ref, qbuf, kbuf, vbuf, abuf, pbuf, *sbufs,
               tiles_per_seq):
    tq, d = x_ref.shape
    kvd = N_KV_HEADS * HEAD_DIM
    nblk = tq // WINDOW
    first = (pl.program_id(0) % tiles_per_seq) == 0

    @pl.when(pl.program_id(0) == 0)
    def _():
        wqkv_ref[...] = wqkv_f32[...].T.astype(BF16)
        wo_ref[...] = wo_f32[...].T.astype(BF16)

    @pl.when(first)
    def _():
        kbuf[0:WINDOW, :] = jnp.zeros((WINDOW, kvd), BF16)
        vbuf[:, 0:WINDOW] = jnp.zeros((kvd, WINDOW), BF16)

    @pl.when(jnp.logical_not(first))
    def _():
        kbuf[0:WINDOW, :] = kbuf[tq:tq + WINDOW, :]
        vbuf[:, 0:WINDOW] = vbuf[:, tq:tq + WINDOW]

    h = _rms(x_ref[...], ng_ref[...]).astype(BF16)

    def proj_step(r0):
        def run():
            pbuf[r0:r0 + MXU_COLS, :] = lax.dot_general(
                wqkv_ref[r0:r0 + MXU_COLS, :], h, (((1,), (1,)), ((), ())),
                preferred_element_type=F32)
        return run

    def head_rows(r0):
        rows = slice(r0, r0 + HEAD_DIM)
        return pbuf[rows, :] + jnp.concatenate([bqkv_ref[rows, :]] * nblk, axis=1)

    def q_norm_step(hd):
        def run():
            qh = head_rows(hd * HEAD_DIM)
            ms = jnp.mean(qh * qh, axis=0, keepdims=True)
            qbuf[hd * HEAD_DIM:(hd + 1) * HEAD_DIM, :] = (
                qh * lax.rsqrt(ms + NORM_EPS)).astype(BF16)
        return run

    def kv_step():
        kn = []
        for g in range(N_KV_HEADS):
            kh = head_rows(d + g * HEAD_DIM)
            ms = jnp.mean(kh * kh, axis=0, keepdims=True)
            kn.append(kh * lax.rsqrt(ms + NORM_EPS))
            vbuf[g * HEAD_DIM:(g + 1) * HEAD_DIM, WINDOW:WINDOW + tq] = (
                head_rows(d + kvd + g * HEAD_DIM).astype(BF16))
        kt = jnp.concatenate(kn, axis=0)
        kbuf[WINDOW:WINDOW + tq, :] = (kt.T * kgain_ref[...]).astype(BF16)

    heads_per_chunk = MXU_COLS // HEAD_DIM
    proj = [proj_step(r0) for r0 in range(0, d + 2 * kvd, MXU_COLS)]
    proj[0]()
    for i in range(d // MXU_COLS):
        proj[i + 1]()
        for hd in range(i * heads_per_chunk, (i + 1) * heads_per_chunk):
            q_norm_step(hd)()
    kv_step()

    kk = lax.broadcasted_iota(jnp.int32, (2 * WINDOW, WINDOW), 0)
    first_key = jnp.where(first, WINDOW, 0)
    ones_rows = (lax.broadcasted_iota(jnp.int32, (BF16_ROWS, 2 * WINDOW), 0) == 0).astype(BF16)
    pairs = [(j, g) for j in range(nblk) for g in range(N_KV_HEADS)]

    def score_steps(n):
        j, g = pairs[n]
        kw = kbuf[j * WINDOW:(j + 2) * WINDOW, g * HEAD_DIM:(g + 1) * HEAD_DIM]

        def step(c0):
            def run():
                qt = jnp.concatenate(
                    [qbuf[(g * GROUP + c0 // WINDOW + i) * HEAD_DIM:
                          (g * GROUP + c0 // WINDOW + i + 1) * HEAD_DIM, j * WINDOW:(j + 1) * WINDOW]
                     for i in range(MXU_COLS // WINDOW)], axis=1)
                sbufs[n % 2][:, c0:c0 + MXU_COLS] = jnp.dot(kw, qt, preferred_element_type=F32)
            return run
        return [step(c0) for c0 in range(0, GROUP * WINDOW, MXU_COLS)]

    def softmax_steps(n):
        j, g = pairs[n]
        cols = slice(j * WINDOW, (j + 1) * WINDOW)
        band = slice(j * WINDOW, (j + 2) * WINDOW)
        va = jnp.concatenate([vbuf[g * HEAD_DIM:(g + 1) * HEAD_DIM, band], ones_rows], axis=0)

        def step(c0):
            def run():
                ps, ms = [], []
                for i in range(MXU_COLS // WINDOW):
                    blk = slice(c0 + i * WINDOW, c0 + (i + 1) * WINDOW)
                    s = sbufs[n % 2][:, blk] + bias_ref[g, :, blk]
                    if j == 0:
                        s = jnp.where(kk >= first_key, s, NEG_INF)
                    sink = sinks_ref[g * GROUP + c0 // WINDOW + i] * LOG2E
                    m = jnp.maximum(jnp.max(s, axis=0, keepdims=True), sink)
                    ps.append(jnp.exp2(s - m).astype(BF16))
                    ms.append((m, sink))
                ot = jnp.dot(va, jnp.concatenate(ps, axis=1), preferred_element_type=F32)
                for i, (m, sink) in enumerate(ms):
                    hd = g * GROUP + c0 // WINDOW + i
                    denom = ot[HEAD_DIM:HEAD_DIM + 1, i * WINDOW:(i + 1) * WINDOW] + jnp.exp2(sink - m)
                    abuf[hd * HEAD_DIM:(hd + 1) * HEAD_DIM, cols] = (
                        ot[0:HEAD_DIM, i * WINDOW:(i + 1) * WINDOW] * (1.0 / denom)).astype(BF16)
            return run
        return [step(c0) for c0 in range(0, GROUP * WINDOW, MXU_COLS)]

    _interleave(score_steps(0), [])
    for n in range(len(pairs)):
        _interleave(softmax_steps(n), score_steps(n + 1) if n + 1 < len(pairs) else [])

    def out_proj_step(r0):
        def run():
            pbuf[r0:r0 + MXU_COLS, :] = jnp.dot(wo_ref[r0:r0 + MXU_COLS, :], abuf[...],
                                                preferred_element_type=F32)
        return run

    def residual_step(r0):
        def run():
            cols = slice(r0, r0 + MXU_COLS)
            o_ref[:, cols] = x_ref[:, cols] + pbuf[cols, :].T + bo_ref[:, cols]
        return run

    out_proj_step(0)()
    for r0 in range(0, d, MXU_COLS):
        if r0 + MXU_COLS < d:
            out_proj_step(r0 + MXU_COLS)()
        residual_step(r0)()


def _attn_block(x, seq, layer, ng, w_qkv, b_qkv, q_g, k_g, sinks, w_o, b_o, bias):
    t, d = x.shape
    qkv_dim = w_qkv.shape[2]
    kvd = N_KV_HEADS * HEAD_DIM
    row = lambda v: v.reshape(1, -1)
    kgain = jnp.tile(q_g * k_g * (LOG2E / math.sqrt(HEAD_DIM)), N_KV_HEADS)
    body = functools.partial(_attn_body, tiles_per_seq=seq // TOKEN_TILE)
    return pl.pallas_call(
        body,
        grid=(t // TOKEN_TILE,),
        in_specs=[
            pl.BlockSpec((TOKEN_TILE, d), lambda i: (i, 0)),
            _const_spec((1, d)),
            _layer_spec((d, qkv_dim), layer),
            _const_spec((qkv_dim, LANES)),
            _const_spec((1, kvd)),
            _const_spec((N_KV_HEADS, 2 * WINDOW, GROUP * WINDOW)),
            pl.BlockSpec(memory_space=pltpu.SMEM),
            _layer_spec((d, d), layer),
            _const_spec((1, d)),
        ],
        out_specs=pl.BlockSpec((TOKEN_TILE, d), lambda i: (i, 0)),
        out_shape=jax.ShapeDtypeStruct((t, d), F32),
        scratch_shapes=[
            pltpu.VMEM((qkv_dim, d), BF16),
            pltpu.VMEM((d, d), BF16),
            pltpu.VMEM((d, TOKEN_TILE), BF16),
            pltpu.VMEM((TOKEN_TILE + WINDOW, kvd), BF16),
            pltpu.VMEM((kvd, TOKEN_TILE + WINDOW), BF16),
            pltpu.VMEM((d, TOKEN_TILE), BF16),
            pltpu.VMEM((qkv_dim, TOKEN_TILE), F32),
            pltpu.VMEM((2 * WINDOW, GROUP * WINDOW), F32),
            pltpu.VMEM((2 * WINDOW, GROUP * WINDOW), F32),
        ],
        compiler_params=_compiler_params(),
        name="attn_block",
    )(x, row(ng), w_qkv, jnp.broadcast_to(b_qkv[:, None], (qkv_dim, LANES)),
      row(kgain), bias, sinks, w_o, row(b_o))


def kernel(x, conv_norm_g, conv_w_in, conv_b_in, conv_dw, conv_dw_b, conv_ln_g, conv_ln_b, conv_w_out, conv_b_out, attn_norm_g, w_qkv, b_qkv, q_norm_g, k_norm_g, sinks, w_o, b_o, rel_bias, mlp_norm_g, w_up, w_down):
    b, s, d = x.shape
    assert s % TOKEN_TILE == 0 and TOKEN_TILE % WINDOW == 0 and d == N_HEADS * HEAD_DIM
    depth = mlp_norm_g.shape[0]
    bias = _band_bias(rel_bias)
    xf = x.reshape(b * s, d)
    for i in range(depth):
        j = i // 2
        if i % 2 == 0:
            xf = _conv_block(xf, s, j, conv_norm_g[j], conv_w_in, conv_b_in[j], conv_dw[j],
                             conv_dw_b[j], conv_ln_g[j], conv_ln_b[j], conv_w_out,
                             conv_b_out[j])
        else:
            xf = _attn_block(xf, s, j, attn_norm_g[j], w_qkv, b_qkv[j], q_norm_g[j],
                             k_norm_g[j], sinks[j], w_o, b_o[j], bias)
        xf = _mlp(xf, mlp_norm_g, w_up, w_down, i)
    return xf.reshape(b, s, d)
```
